```python
import math
import jax, jax.numpy as jnp
from jax import lax
import numpy as np

D_MODEL = 1024
BATCH = 8
SEQ = 4096
DEPTH = 4

ATTN_HEADS = 8
ATTN_HEAD_DIM = 64
ATTN_V_DIM = 2 * ATTN_HEAD_DIM
ATTN_WIDTH = ATTN_HEADS * ATTN_V_DIM
Q_BLOCK = 128
SSD_EXPAND = 2
SSD_WIDTH = SSD_EXPAND * D_MODEL
SSD_HEAD_DIM = 64
SSD_HEADS = SSD_WIDTH // SSD_HEAD_DIM
SSD_GROUPS = 4
SSD_HEADS_PER_GROUP = SSD_HEADS // SSD_GROUPS
SSD_STATE = 128
SSD_CONV = 4
SSD_CHUNK = 128
CONV_DIM = SSD_WIDTH + 2 * SSD_GROUPS * SSD_STATE
DT_MIN = 0.001
DT_MAX = 0.1
N_BRANCHES = 2
IN_SPLITS = (ATTN_WIDTH, ATTN_WIDTH, ATTN_WIDTH, ATTN_WIDTH,
             CONV_DIM, SSD_WIDTH, SSD_HEADS,
             N_BRANCHES * D_MODEL)
D_IN_PROJ = sum(IN_SPLITS)
SPLIT_POINTS = [int(i) for i in np.cumsum(IN_SPLITS)[:-1]]
EPS = 1e-6

kernel_name = "hybrid_diffattn_ssd_gated_merge"


def rms_norm(x, w):
    xf = x.astype(jnp.float32)
    y = xf * lax.rsqrt(jnp.mean(xf * xf, axis=-1, keepdims=True) + EPS)
    return (y * w.astype(jnp.float32)).astype(x.dtype)


def lambda_init_fn(layer_idx):
    return 0.8 - 0.6 * math.exp(-0.3 * layer_idx)


def diff_attention(q, k, v, q_norm_w, k_norm_w, lam, subln_w, lambda_init):
    b, s = q.shape[0], q.shape[1]
    q = rms_norm(q, q_norm_w)
    k = rms_norm(k, k_norm_w)
    scale = ATTN_HEAD_DIM ** -0.5
    n_blk = s // Q_BLOCK
    q_blocks = q.reshape(b, n_blk, Q_BLOCK, ATTN_HEADS, 2, ATTN_HEAD_DIM).transpose(1, 0, 2, 3, 4, 5)
    key_pos = jnp.arange(s)

    def one_block(args):
        qb, i = args
        q_pos = i * Q_BLOCK + jnp.arange(Q_BLOCK)
        scores = jnp.einsum('bqhmd,bkhmd->bhmqk', qb, k).astype(jnp.float32) * scale
        causal = key_pos[None, :] <= q_pos[:, None]
        scores = jnp.where(causal, scores, -jnp.inf)
        p = jax.nn.softmax(scores, axis=-1)
        a = p[:, :, 0] - lam * p[:, :, 1]
        return jnp.einsum('bhqk,bkhe->bqhe', a.astype(v.dtype), v)

    o = lax.map(one_block, (q_blocks, jnp.arange(n_blk)))
    o = o.transpose(1, 0, 2, 3, 4).reshape(b, s, ATTN_HEADS, ATTN_V_DIM)
    o = rms_norm(o, subln_w) * (1.0 - lambda_init)
    return o.reshape(b, s, ATTN_WIDTH)


def causal_depthwise_conv(x, w, bias):
    y = lax.conv_general_dilated(
        x, w[:, None, :], window_strides=(1,), padding=[(SSD_CONV - 1, 0)],
        dimension_numbers=('NWC', 'WIO', 'NWC'), feature_group_count=x.shape[-1])
    return y + bias


def ssd_chunked(x, dt, A, B, C, D):
    b, s = x.shape[0], x.shape[1]
    nc, L, G, J = s // SSD_CHUNK, SSD_CHUNK, SSD_GROUPS, SSD_HEADS_PER_GROUP
    xf = x.astype(jnp.float32).reshape(b, nc, L, G, J, SSD_HEAD_DIM)
    dtc = dt.astype(jnp.float32).reshape(b, nc, L, G, J)
    Bc = B.astype(jnp.float32).reshape(b, nc, L, G, SSD_STATE)
    Cc = C.astype(jnp.float32).reshape(b, nc, L, G, SSD_STATE)
    a_cum = jnp.cumsum(dtc * A.astype(jnp.float32).reshape(G, J), axis=2)
    xdt = xf * dtc[..., None]
    a_t = a_cum.transpose(0, 1, 3, 4, 2)
    seg = a_t[..., :, None] - a_t[..., None, :]
    causal = jnp.tril(jnp.ones((L, L), dtype=bool))
    decay = jnp.exp(jnp.where(causal, seg, -jnp.inf))
    cb = jnp.einsum('bclgn,bcsgn->bcgls', Cc, Bc)
    y_diag = jnp.einsum('bcgjls,bcsgjp->bclgjp', cb[:, :, :, None] * decay, xdt)
    decay_states = jnp.exp(a_cum[:, :, -1:] - a_cum)
    states = jnp.einsum('bclgn,bclgjp->bcgjpn', Bc, xdt * decay_states[..., None])
    chunk_decay = jnp.exp(a_cum[:, :, -1])

    def step(h, inp):
        st, dec = inp
        return h * dec[..., None, None] + st, h

    h0 = jnp.zeros((b, G, J, SSD_HEAD_DIM, SSD_STATE), jnp.float32)
    _, prev = lax.scan(step, h0, (states.transpose(1, 0, 2, 3, 4, 5), chunk_decay.transpose(1, 0, 2, 3)))
    prev = prev.transpose(1, 0, 2, 3, 4, 5)
    y_off = jnp.einsum('bclgn,bcgjpn->bclgjp', Cc, prev) * jnp.exp(a_cum)[..., None]
    y = (y_diag + y_off).reshape(b, s, SSD_HEADS, SSD_HEAD_DIM)
    y = y + xf.reshape(b, s, SSD_HEADS, SSD_HEAD_DIM) * D.astype(jnp.float32)[:, None]
    return y.reshape(b, s, SSD_WIDTH).astype(x.dtype)


def gated_group_rms_norm(y, z, w):
    b, s = y.shape[0], y.shape[1]
    g = (y * jax.nn.silu(z)).reshape(b, s, SSD_GROUPS, SSD_WIDTH // SSD_GROUPS)
    gf = g.astype(jnp.float32)
    gf = gf * lax.rsqrt(jnp.mean(gf * gf, axis=-1, keepdims=True) + EPS)
    return (gf.reshape(b, s, SSD_WIDTH) * w.astype(jnp.float32)).astype(y.dtype)


def hybrid_layer(x, layer_idx, norm_w, w_in, q_norm_w, k_norm_w, diff_lambda, subln_w,
                 conv_w, conv_b, dt_bias, a_log, d_skip, ssd_norm_w,
                 w_proj_attn, w_proj_ssd, w_out):
    b, s = x.shape[0], x.shape[1]
    h = rms_norm(x, norm_w)
    proj = h @ w_in
    q, k, v, z_a, xbc, z_s, dt_raw, gate_logits = jnp.split(proj, SPLIT_POINTS, axis=-1)

    lambda_init = lambda_init_fn(layer_idx)
    lf = diff_lambda.astype(jnp.float32)
    lam = jnp.exp(jnp.sum(lf[0] * lf[1])) - jnp.exp(jnp.sum(lf[2] * lf[3])) + lambda_init
    y_a = diff_attention(q.reshape(b, s, ATTN_HEADS, 2, ATTN_HEAD_DIM),
                         k.reshape(b, s, ATTN_HEADS, 2, ATTN_HEAD_DIM),
                         v.reshape(b, s, ATTN_HEADS, ATTN_V_DIM),
                         q_norm_w, k_norm_w, lam, subln_w, lambda_init)
    y_a = y_a * jax.nn.silu(z_a)

    xbc = jax.nn.silu(causal_depthwise_conv(xbc, conv_w, conv_b))
    xs, Bm, Cm = jnp.split(xbc, [SSD_WIDTH, SSD_WIDTH + SSD_GROUPS * SSD_STATE], axis=-1)
    dt = jax.nn.softplus(dt_raw.astype(jnp.float32) + dt_bias.astype(jnp.float32))
    A = -jnp.exp(a_log.astype(jnp.float32))
    y_s = ssd_chunked(xs.reshape(b, s, SSD_HEADS, SSD_HEAD_DIM), dt, A,
                      Bm.reshape(b, s, SSD_GROUPS, SSD_STATE), Cm.reshape(b, s, SSD_GROUPS, SSD_STATE), d_skip)
    y_s = gated_group_rms_norm(y_s, z_s, ssd_norm_w)

    g_a, g_s = jnp.split(gate_logits, N_BRANCHES, axis=-1)
    merged = jax.nn.sigmoid(g_a) * (y_a @ w_proj_attn) + jax.nn.sigmoid(g_s) * (y_s @ w_proj_ssd)
    return x + (merged @ w_out).astype(x.dtype)


def setup_inputs(seed: int = 0) -> dict:
    key = jax.random.key(seed)
    ks = jax.random.split(key, 20)
    f32 = jnp.float32
    nrm = lambda k, shape, sc: jax.random.normal(k, shape, f32) * sc
    dt0 = jnp.exp(jax.random.uniform(ks[9], (DEPTH, SSD_HEADS), f32) * (math.log(DT_MAX) - math.log(DT_MIN)) + math.log(DT_MIN))
    dt0 = jnp.maximum(dt0, 1e-4)
    return {
        "x": nrm(ks[0], (BATCH, SEQ, D_MODEL), 1.0),
        "norm_w": 1.0 + nrm(ks[1], (DEPTH, D_MODEL), 0.02),
        "w_in": nrm(ks[2], (DEPTH, D_MODEL, D_IN_PROJ), D_MODEL ** -0.5),
        "q_norm_w": 1.0 + nrm(ks[3], (DEPTH, ATTN_HEAD_DIM), 0.02),
        "k_norm_w": 1.0 + nrm(ks[4], (DEPTH, ATTN_HEAD_DIM), 0.02),
        "diff_lambda": nrm(ks[5], (DEPTH, 4, ATTN_HEAD_DIM), 0.1),
        "subln_w": 1.0 + nrm(ks[6], (DEPTH, ATTN_V_DIM), 0.02),
        "conv_w": nrm(ks[7], (DEPTH, SSD_CONV, CONV_DIM), SSD_CONV ** -0.5),
        "conv_b": nrm(ks[8], (DEPTH, CONV_DIM), 0.02),
        "dt_bias": dt0 + jnp.log(-jnp.expm1(-dt0)),
        "a_log": jnp.log(jax.random.uniform(ks[10], (DEPTH, SSD_HEADS), f32, 1.0, 16.0)),
        "d_skip": 1.0 + nrm(ks[11], (DEPTH, SSD_HEADS), 0.02),
        "ssd_norm_w": 1.0 + nrm(ks[12], (DEPTH, SSD_WIDTH), 0.02),
        "w_proj_attn": nrm(ks[13], (DEPTH, ATTN_WIDTH, D_MODEL), ATTN_WIDTH ** -0.5),
        "w_proj_ssd": nrm(ks[14], (DEPTH, SSD_WIDTH, D_MODEL), SSD_WIDTH ** -0.5),
        "w_out": nrm(ks[15], (DEPTH, D_MODEL, D_MODEL), D_MODEL ** -0.5),
    }


def reference(x, norm_w, w_in, q_norm_w, k_norm_w, diff_lambda, subln_w, conv_w, conv_b,
              dt_bias, a_log, d_skip, ssd_norm_w, w_proj_attn, w_proj_ssd, w_out):
    for l in range(DEPTH):
        x = hybrid_layer(x, l, norm_w[l], w_in[l], q_norm_w[l], k_norm_w[l], diff_lambda[l], subln_w[l],
                         conv_w[l], conv_b[l], dt_bias[l], a_log[l], d_skip[l], ssd_norm_w[l],
                         w_proj_attn[l], w_proj_ssd[l], w_out[l])
    return x
```

```python
import functools
import math

import jax
import jax.numpy as jnp
from jax import lax
from jax.experimental import pallas as pl
from jax.experimental.pallas import tpu as pltpu

F32 = jnp.float32
BF16 = jnp.bfloat16

EPS = 1e-6
LANES = 128
ATTN_HEAD_DIM = 64
ATTN_V_DIM = 2 * ATTN_HEAD_DIM
SSD_HEAD_DIM = 64
SSD_GROUPS = 4
SSD_STATE = 128
SSD_CHUNK = 128
SSD_CONV = 4
CONV_HIST = 8
NEG_BIG = -1e30
LOG2E = math.log2(math.e)

NT_DIMS = (((1,), (1,)), ((), ()))


def _compiler_params(semantics, vmem_mib):
    return pltpu.CompilerParams(dimension_semantics=semantics,
                                vmem_limit_bytes=vmem_mib * 1024 * 1024)


def _silu(x):
    return x * jax.nn.sigmoid(x)


def _inproj_kernel(x_ref, nw_ref, w_ref, wdt_ref, dtb_ref, proj_ref, dt_ref, h_ref, *, row_chunk):
    j = pl.program_id(1)

    @pl.when(j == 0)
    def _():
        tm = x_ref.shape[0]
        for r in range(tm // row_chunk):
            rows = pl.ds(r * row_chunk, row_chunk)
            x = x_ref[rows, :]
            ms = jnp.mean(x * x, axis=-1, keepdims=True)
            h = (x * lax.rsqrt(ms + EPS) * nw_ref[...]).astype(BF16)
            h_ref[rows, :] = h
            raw = jnp.dot(h, wdt_ref[...], preferred_element_type=F32) + dtb_ref[...]
            dt_ref[rows, :] = jnp.maximum(raw, 0.0) + jnp.log1p(jnp.exp(-jnp.abs(raw)))

    proj_ref[...] = jnp.dot(h_ref[...], w_ref[...], preferred_element_type=F32).astype(proj_ref.dtype)


def _in_proj(x, norm_w, w_main, w_dt, dt_bias, *, tm, tn):
    m, d = x.shape
    n = w_main.shape[1]
    grid = (m // tm, n // tn)
    return pl.pallas_call(
        functools.partial(_inproj_kernel, row_chunk=256),
        grid=grid,
        in_specs=[
            pl.BlockSpec((tm, d), lambda i, j: (i, 0)),
            pl.BlockSpec((1, d), lambda i, j: (0, 0)),
            pl.BlockSpec((d, tn), lambda i, j: (0, j)),
            pl.BlockSpec((d, LANES), lambda i, j: (0, 0)),
            pl.BlockSpec((1, LANES), lambda i, j: (0, 0)),
        ],
        out_specs=[
            pl.BlockSpec((tm, tn), lambda i, j: (i, j)),
            pl.BlockSpec((tm, LANES), lambda i, j: (i, 0)),
        ],
        out_shape=[
            jax.ShapeDtypeStruct((m, n), BF16),
            jax.ShapeDtypeStruct((m, LANES), F32),
        ],
        scratch_shapes=[pltpu.VMEM((tm, d), BF16)],
        compiler_params=_compiler_params(("parallel", "arbitrary"), 48),
        name="in_proj",
    )(x, norm_w, w_main, w_dt, dt_bias)


def _half_rms(x, w):
    lane = lax.broadcasted_iota(jnp.int32, x.shape, 1)
    left = lane < ATTN_HEAD_DIM
    x2 = x * x
    s_left = jnp.sum(jnp.where(left, x2, 0.0), axis=-1, keepdims=True)
    s_right = jnp.sum(jnp.where(left, 0.0, x2), axis=-1, keepdims=True)
    ms = jnp.where(left, s_left, s_right) * (1.0 / ATTN_HEAD_DIM)
    return x * lax.rsqrt(ms + EPS) * w


def _qkv_prep_kernel(q_ref, k_ref, v_ref, qw_ref, kw_ref, qn_ref, kn_ref, vt_ref, *, q_scale):
    qn = _half_rms(q_ref[...].astype(F32), qw_ref[...]) * q_scale
    qn_ref[...] = qn.astype(qn_ref.dtype)
    kn_ref[...] = _half_rms(k_ref[...].astype(F32), kw_ref[...]).astype(kn_ref.dtype)
    vt_ref[0, 0, 0] = v_ref[...].astype(F32).T.astype(vt_ref.dtype)


def _qkv_prep(proj, qw, kw, *, batch, seq, heads, ts):
    m = proj.shape[0]
    ns = seq // ts
    width = heads * ATTN_V_DIM
    q_scale = ATTN_HEAD_DIM ** -0.5 * LOG2E
    row = lambda b, h, s: b * ns + s
    return pl.pallas_call(
        functools.partial(_qkv_prep_kernel, q_scale=q_scale),
        grid=(batch, heads, ns),
        in_specs=[
            pl.BlockSpec((ts, ATTN_V_DIM), lambda b, h, s: (row(b, h, s), h)),
            pl.BlockSpec((ts, ATTN_V_DIM), lambda b, h, s: (row(b, h, s), heads + h)),
            pl.BlockSpec((ts, ATTN_V_DIM), lambda b, h, s: (row(b, h, s), 2 * heads + h)),
            pl.BlockSpec((1, ATTN_V_DIM), lambda b, h, s: (0, 0)),
            pl.BlockSpec((1, ATTN_V_DIM), lambda b, h, s: (0, 0)),
        ],
        out_specs=[
            pl.BlockSpec((ts, ATTN_V_DIM), lambda b, h, s: (row(b, h, s), h)),
            pl.BlockSpec((ts, ATTN_V_DIM), lambda b, h, s: (row(b, h, s), h)),
            pl.BlockSpec((1, 1, 1, ATTN_V_DIM, ts), lambda b, h, s: (b, h, s, 0, 0)),
        ],
        out_shape=[
            jax.ShapeDtypeStruct((m, width), BF16),
            jax.ShapeDtypeStruct((m, width), BF16),
            jax.ShapeDtypeStruct((batch, heads, ns, ATTN_V_DIM, ts), BF16),
        ],
        compiler_params=_compiler_params(("parallel", "parallel", "parallel"), 32),
        name="qkv_prep",
    )(proj, proj, proj, qw, kw)


def _attn_kernel(lam_ref, q_ref, k_ref, vt_ref, z_ref, sw_ref, o_ref, m_ref, l_ref, acc_ref,
                 *, lambda_init):
    qi = pl.program_id(2)
    tq = q_ref.shape[0]
    q = q_ref[...]
    lane = lax.broadcasted_iota(jnp.int32, q.shape, 1)
    zero = jnp.zeros_like(q)
    q_maps = (jnp.where(lane < ATTN_HEAD_DIM, q, zero), jnp.where(lane < ATTN_HEAD_DIM, zero, q))

    m_ref[...] = jnp.full(m_ref.shape, NEG_BIG, F32)
    l_ref[...] = jnp.zeros(l_ref.shape, F32)
    acc_ref[...] = jnp.zeros(acc_ref.shape, F32)

    def tile(kj, masked):
        k = k_ref[0, kj]
        vt = vt_ref[0, 0, kj]
        for mp in range(2):
            s = lax.dot_general(k, q_maps[mp], NT_DIMS, preferred_element_type=F32)
            if masked:
                key = lax.broadcasted_iota(jnp.int32, s.shape, 0)
                qry = lax.broadcasted_iota(jnp.int32, s.shape, 1)
                s = jnp.where(key <= qry, s, NEG_BIG)
            m_old = m_ref[mp]
            m_new = jnp.maximum(m_old, jnp.max(s, axis=0, keepdims=True))
            p = jnp.exp2(s - m_new)
            alpha = jnp.exp2(m_old - m_new)
            l_ref[mp] = alpha * l_ref[mp] + jnp.sum(p, axis=0, keepdims=True)
            acc_ref[mp] = alpha * acc_ref[mp] + jnp.dot(vt, p.astype(BF16), preferred_element_type=F32)
            m_ref[mp] = m_new

    def body(kj, carry):
        tile(kj, False)
        return carry

    lax.fori_loop(0, qi, body, 0)
    tile(qi, True)

    lf = lam_ref[...]
    s01 = jnp.sum(lf[0:1] * lf[1:2], axis=-1, keepdims=True)
    s23 = jnp.sum(lf[2:3] * lf[3:4], axis=-1, keepdims=True)
    lam = jnp.exp(s01) - jnp.exp(s23) + lambda_init
    o = acc_ref[0] / l_ref[0] - lam * (acc_ref[1] / l_ref[1])
    ms = jnp.mean(o * o, axis=0, keepdims=True)
    on = (o * lax.rsqrt(ms + EPS)).T
    z = z_ref[...].astype(F32)
    y = on * sw_ref[...] * (1.0 - lambda_init) * _silu(z)
    o_ref[...] = y.astype(o_ref.dtype)


def _diff_attn(diff_lambda, qn, kn, vt, proj, subln_w, *, batch, seq, heads, tq, lambda_init, z_col0):
    m, width = qn.shape
    nq = seq // tq
    kn4 = kn.reshape(batch, nq, tq, width)
    row = lambda b, h, i: b * nq + i
    return pl.pallas_call(
        functools.partial(_attn_kernel, lambda_init=lambda_init),
        grid=(batch, heads, nq),
        in_specs=[
            pl.BlockSpec(diff_lambda.shape, lambda b, h, i: (0, 0)),
            pl.BlockSpec((tq, ATTN_V_DIM), lambda b, h, i: (row(b, h, i), h)),
            pl.BlockSpec((1, nq, tq, ATTN_V_DIM), lambda b, h, i: (b, 0, 0, h)),
            pl.BlockSpec((1, 1, nq, ATTN_V_DIM, tq), lambda b, h, i: (b, h, 0, 0, 0)),
            pl.BlockSpec((tq, ATTN_V_DIM), lambda b, h, i: (row(b, h, i), z_col0 + h)),
            pl.BlockSpec((1, ATTN_V_DIM), lambda b, h, i: (0, 0)),
        ],
        out_specs=pl.BlockSpec((tq, ATTN_V_DIM), lambda b, h, i: (row(b, h, i), h)),
        out_shape=jax.ShapeDtypeStruct((m, width), BF16),
        scratch_shapes=[
            pltpu.VMEM((2, 1, tq), F32),
            pltpu.VMEM((2, 1, tq), F32),
            pltpu.VMEM((2, ATTN_V_DIM, tq), F32),
        ],
        compiler_params=_compiler_params(("parallel", "parallel", "arbitrary"), 48),
        name="diff_attn",
    )(diff_lambda, qn, kn4, vt, proj, subln_w)


def _causal_conv_silu(cur_ref, hist_ref, ext_ref, w_ref, b_ref):
    rows = cur_ref.shape[0]
    cur = cur_ref[...].astype(F32)
    ext_ref[0:CONV_HIST, :] = hist_ref[...]
    ext_ref[CONV_HIST:CONV_HIST + rows, :] = cur
    hist_ref[...] = cur[rows - CONV_HIST:rows, :]
    acc = b_ref[...] + w_ref[SSD_CONV - 1:SSD_CONV, :] * cur
    for k in range(SSD_CONV - 1):
        start = CONV_HIST - (SSD_CONV - 1) + k
        acc = acc + w_ref[k:k + 1, :] * ext_ref[start:start + rows, :]
    return _silu(acc)


def _split3(x):
    hi = x.astype(BF16)
    r1 = x - hi.astype(F32)
    mid = r1.astype(BF16)
    lo = (r1 - mid.astype(F32)).astype(BF16)
    return hi, mid, lo


def _ssd_kernel(xs_ref, b_ref, c_ref, zs_ref, dt_ref,
                cwx_ref, cbx_ref, cwb_ref, cbb_ref, cwc_ref, cbc_ref,
                alog_ref, dskip_ref, nw_ref,
                y_ref,
                hx_ref, hb_ref, hc_ref, ex_ref, eb_ref, ec_ref, state_ref):
    c = pl.program_id(1)
    L = xs_ref.shape[0]
    N = SSD_STATE
    pair_w = 2 * SSD_HEAD_DIM
    n_pairs = xs_ref.shape[1] // pair_w
    pairs_per_group = n_pairs // SSD_GROUPS

    @pl.when(c == 0)
    def _():
        hx_ref[...] = jnp.zeros(hx_ref.shape, F32)
        hb_ref[...] = jnp.zeros(hb_ref.shape, F32)
        hc_ref[...] = jnp.zeros(hc_ref.shape, F32)
        state_ref[...] = jnp.zeros(state_ref.shape, F32)

    xs = _causal_conv_silu(xs_ref, hx_ref, ex_ref, cwx_ref, cbx_ref)
    bm = _causal_conv_silu(b_ref, hb_ref, eb_ref, cwb_ref, cbb_ref)
    cm = _causal_conv_silu(c_ref, hc_ref, ec_ref, cwc_ref, cbc_ref)

    dt = dt_ref[...]
    a_neg = -jnp.exp(alog_ref[...])
    dta = dt * a_neg
    row = lax.broadcasted_iota(jnp.int32, (L, L), 0)
    col = lax.broadcasted_iota(jnp.int32, (L, L), 1)
    causal = row >= col
    tri = jnp.where(causal, 1.0, 0.0).astype(BF16)
    a_cum = sum(jnp.dot(tri, part, preferred_element_type=F32) for part in _split3(dta))
    a_cum_t = a_cum.T
    dt_t = dt.T

    xs_bf = xs.astype(BF16)
    lane = lax.broadcasted_iota(jnp.int32, (L, pair_w), 1)
    left = lane < SSD_HEAD_DIM

    y_slabs = []
    for g in range(SSD_GROUPS):
        bg = bm[:, g * N:(g + 1) * N]
        cg = cm[:, g * N:(g + 1) * N]
        cb = lax.dot_general(cg.astype(BF16), bg.astype(BF16), NT_DIMS, preferred_element_type=F32)
        bg_t = bg.T
        for jp in range(pairs_per_group):
            pr = g * pairs_per_group + jp
            lanes = slice(pr * pair_w, (pr + 1) * pair_w)
            x_pair = xs_bf[:, lanes]
            s_pair = state_ref[:, lanes]
            lhs_y, lhs_s, chunk_decay = [], [], []
            for e in range(2):
                h = 2 * pr + e
                colb = jnp.broadcast_to(a_cum[:, h:h + 1], (L, L))
                rowb = a_cum_t[h:h + 1, :]
                dtrow = dt_t[h:h + 1, :]
                decay = jnp.exp(jnp.where(causal, colb - rowb, NEG_BIG))
                m_h = cb * decay * dtrow
                cw_h = cg * jnp.exp(colb)
                lhs_y.append(jnp.concatenate([m_h, cw_h], axis=1).astype(BF16))
                a_last = colb[L - 1:L, :]
                wrow = dtrow * jnp.exp(a_last - rowb)
                lhs_s.append((bg_t * wrow).astype(BF16))
                chunk_decay.append(jnp.exp(a_last))
            rhs_y = jnp.concatenate([x_pair, s_pair.astype(BF16)], axis=0)
            y2 = jnp.dot(jnp.concatenate(lhs_y, axis=0), rhs_y, preferred_element_type=F32)
            y_slabs.append(jnp.where(left, y2[:L], y2[L:]))
            s2 = jnp.dot(jnp.concatenate(lhs_s, axis=0), x_pair, preferred_element_type=F32)
            state_ref[:, lanes] = (jnp.where(left, s2[:N], s2[N:])
                                   + s_pair * jnp.where(left, chunk_decay[0], chunk_decay[1]))

    y = jnp.concatenate(y_slabs, axis=1) + xs * dskip_ref[...]
    gated = y * _silu(zs_ref[...].astype(F32))
    gw = gated.shape[1] // SSD_GROUPS
    for g in range(SSD_GROUPS):
        blk = gated[:, g * gw:(g + 1) * gw]
        ms = jnp.mean(blk * blk, axis=-1, keepdims=True)
        y_ref[:, g * gw:(g + 1) * gw] = (blk * lax.rsqrt(ms + EPS) * nw_ref[:, g * gw:(g + 1) * gw]).astype(y_ref.dtype)


def _ssd(proj, dt, conv_wx, conv_bx, conv_wb, conv_bb, conv_wc, conv_bc, a_log, d_skip, norm_w,
         *, batch, seq, xs_col, b_col, c_col, zs_col):
    m = proj.shape[0]
    L = SSD_CHUNK
    nc = seq // L
    width = conv_wx.shape[1]
    gn = conv_wb.shape[1]
    row = lambda b, c: b * nc + c
    const = lambda b, c: (0, 0)
    return pl.pallas_call(
        _ssd_kernel,
        grid=(batch, nc),
        in_specs=[
            pl.BlockSpec((L, width), lambda b, c: (row(b, c), xs_col)),
            pl.BlockSpec((L, gn), lambda b, c: (row(b, c), b_col)),
            pl.BlockSpec((L, gn), lambda b, c: (row(b, c), c_col)),
            pl.BlockSpec((L, width), lambda b, c: (row(b, c), zs_col)),
            pl.BlockSpec((L, LANES), lambda b, c: (row(b, c), 0)),
            pl.BlockSpec(conv_wx.shape, const), pl.BlockSpec(conv_bx.shape, const),
            pl.BlockSpec(conv_wb.shape, const), pl.BlockSpec(conv_bb.shape, const),
            pl.BlockSpec(conv_wc.shape, const), pl.BlockSpec(conv_bc.shape, const),
            pl.BlockSpec(a_log.shape, const), pl.BlockSpec(d_skip.shape, const),
            pl.BlockSpec(norm_w.shape, const),
        ],
        out_specs=pl.BlockSpec((L, width), lambda b, c: (row(b, c), 0)),
        out_shape=jax.ShapeDtypeStruct((m, width), BF16),
        scratch_shapes=[
            pltpu.VMEM((CONV_HIST, width), F32), pltpu.VMEM((CONV_HIST, gn), F32), pltpu.VMEM((CONV_HIST, gn), F32),
            pltpu.VMEM((CONV_HIST + L, width), F32), pltpu.VMEM((CONV_HIST + L, gn), F32),
            pltpu.VMEM((CONV_HIST + L, gn), F32),
            pltpu.VMEM((SSD_STATE, width), F32),
        ],
        compiler_params=_compiler_params(("parallel", "arbitrary"), 48),
        name="ssd",
    )(proj, proj, proj, proj, dt, conv_wx, conv_bx, conv_wb, conv_bb, conv_wc, conv_bc, a_log, d_skip, norm_w)


def _outproj_kernel(ya_ref, ys_ref, ga_ref, gs_ref, x_ref, wpa_ref, wps_ref, wo_ref, o_ref):
    pa = jnp.dot(ya_ref[...], wpa_ref[...], preferred_element_type=F32)
    ps = jnp.dot(ys_ref[...], wps_ref[...], preferred_element_type=F32)
    merged = (jax.nn.sigmoid(ga_ref[...].astype(F32)) * pa
              + jax.nn.sigmoid(gs_ref[...].astype(F32)) * ps)
    o_ref[...] = x_ref[...] + jnp.dot(merged.astype(BF16), wo_ref[...], preferred_element_type=F32)


def _out_proj(y_a, y_s, proj, x, w_pa, w_ps, w_o, *, tm, ga_col, gs_col):
    m, d = x.shape
    const = lambda i: (0, 0)
    return pl.pallas_call(
        _outproj_kernel,
        grid=(m // tm,),
        in_specs=[
            pl.BlockSpec((tm, y_a.shape[1]), lambda i: (i, 0)),
            pl.BlockSpec((tm, y_s.shape[1]), lambda i: (i, 0)),
            pl.BlockSpec((tm, d), lambda i: (i, ga_col)),
            pl.BlockSpec((tm, d), lambda i: (i, gs_col)),
            pl.BlockSpec((tm, d), lambda i: (i, 0)),
            pl.BlockSpec(w_pa.shape, const),
            pl.BlockSpec(w_ps.shape, const),
            pl.BlockSpec(w_o.shape, const),
        ],
        out_specs=pl.BlockSpec((tm, d), lambda i: (i, 0)),
        out_shape=jax.ShapeDtypeStruct((m, d), F32),
        compiler_params=_compiler_params(("parallel",), 48),
        name="out_proj",
    )(y_a, y_s, proj, proj, x, w_pa, w_ps, w_o)


def _lambda_init(layer_idx):
    return 0.8 - 0.6 * math.exp(-0.3 * layer_idx)


def kernel(x, norm_w, w_in, q_norm_w, k_norm_w, diff_lambda, subln_w, conv_w, conv_b, dt_bias, a_log, d_skip,
           ssd_norm_w, w_proj_attn, w_proj_ssd, w_out):
    batch, seq, d_model = x.shape
    depth = w_in.shape[0]
    attn_w = w_proj_attn.shape[1]
    ssd_w = w_proj_ssd.shape[1]
    ssd_heads = dt_bias.shape[1]
    heads = attn_w // ATTN_V_DIM
    gn = SSD_GROUPS * SSD_STATE

    o_za = 3 * attn_w
    o_xs = 4 * attn_w
    o_b = o_xs + ssd_w
    o_c = o_b + gn
    o_zs = o_c + gn
    o_dt = o_zs + ssd_w
    o_gate = o_dt + ssd_heads
    w_main = jnp.concatenate([w_in[:, :, :o_b], w_in[:, :, o_zs:o_dt], w_in[:, :, o_gate:],
                              w_in[:, :, o_b:o_zs]], axis=-1).astype(BF16)
    w_dt = jnp.pad(w_in[:, :, o_dt:o_gate], ((0, 0), (0, 0), (0, LANES - ssd_heads))).astype(BF16)
    pad_h = ((0, 0), (0, LANES - ssd_heads))
    dt_bias_p = jnp.pad(dt_bias, pad_h)[:, None, :]
    a_log_p = jnp.pad(a_log, pad_h)[:, None, :]
    d_skip_x = jnp.repeat(d_skip, SSD_HEAD_DIM, axis=-1)[:, None, :]
    qw = jnp.tile(q_norm_w, (1, 2))[:, None, :]
    kw = jnp.tile(k_norm_w, (1, 2))[:, None, :]
    w_pa = w_proj_attn.astype(BF16)
    w_ps = w_proj_ssd.astype(BF16)
    w_o = w_out.astype(BF16)

    c_xs = o_xs // ssd_w
    c_zs = (o_xs + ssd_w) // ssd_w
    c_ga = (o_xs + 2 * ssd_w) // d_model
    c_gs = c_ga + 1
    c_b = (o_xs + 2 * ssd_w + 2 * d_model) // gn
    c_c = c_b + 1
    c_za = o_za // ATTN_V_DIM

    tq = min(512, seq)
    xf = x.reshape(batch * seq, d_model)
    for l in range(depth):
        proj, dt = _in_proj(xf, norm_w[l][None, :], w_main[l], w_dt[l], dt_bias_p[l], tm=1024, tn=1024)
        qn, kn, vt = _qkv_prep(proj, qw[l], kw[l], batch=batch, seq=seq, heads=heads, ts=tq)
        y_a = _diff_attn(diff_lambda[l], qn, kn, vt, proj, subln_w[l][None, :], batch=batch, seq=seq, heads=heads,
                         tq=tq, lambda_init=_lambda_init(l), z_col0=c_za)
        y_s = _ssd(proj, dt, conv_w[l][:, :ssd_w], conv_b[l][None, :ssd_w],
                   conv_w[l][:, ssd_w:ssd_w + gn], conv_b[l][None, ssd_w:ssd_w + gn],
                   conv_w[l][:, ssd_w + gn:], conv_b[l][None, ssd_w + gn:],
                   a_log_p[l], d_skip_x[l], ssd_norm_w[l][None, :],
                   batch=batch, seq=seq, xs_col=c_xs, b_col=c_b, c_col=c_c, zs_col=c_zs)
        xf = _out_proj(y_a, y_s, proj, xf, w_pa[l], w_ps[l], w_o[l], tm=512, ga_col=c_ga, gs_col=c_gs)
    return xf.reshape(batch, seq, d_model)
```

```python
import functools
import math

import jax
import jax.numpy as jnp
from jax import lax
from jax.experimental import pallas as pl
from jax.experimental.pallas import tpu as pltpu

F32 = jnp.float32
BF16 = jnp.bfloat16

EPS = 1e-6
LANES = 128
ATTN_HEAD_DIM = 64
ATTN_V_DIM = 2 * ATTN_HEAD_DIM
SSD_HEAD_DIM = 64
SSD_GROUPS = 4
SSD_STATE = 128
SSD_CHUNK = 128
SSD_CONV = 4
CONV_HIST = 8
NEG_BIG = -1e30
LOG2E = math.log2(math.e)

NT_DIMS = (((1,), (1,)), ((), ()))


def _compiler_params(semantics, vmem_mib):
    return pltpu.CompilerParams(dimension_semantics=semantics,
                                vmem_limit_bytes=vmem_mib * 1024 * 1024)


def _silu(x):
    return x * jax.nn.sigmoid(x)


def _inproj_kernel(x_ref, nw_ref, w_ref, wdt_ref, dtb_ref, proj_ref, dt_ref, h_ref, *, row_chunk):
    j = pl.program_id(1)

    @pl.when(j == 0)
    def _():
        tm = x_ref.shape[0]
        for r in range(tm // row_chunk):
            rows = pl.ds(r * row_chunk, row_chunk)
            x = x_ref[rows, :]
            ms = jnp.mean(x * x, axis=-1, keepdims=True)
            h = (x * lax.rsqrt(ms + EPS) * nw_ref[...]).astype(BF16)
            h_ref[rows, :] = h
            raw = jnp.dot(h, wdt_ref[...], preferred_element_type=F32) + dtb_ref[...]
            dt_ref[rows, :] = jnp.maximum(raw, 0.0) + jnp.log1p(jnp.exp(-jnp.abs(raw)))

    proj_ref[...] = jnp.dot(h_ref[...], w_ref[...], preferred_element_type=F32).astype(proj_ref.dtype)


def _in_proj(x, norm_w, w_main, w_dt, dt_bias, *, tm, tn):
    m, d = x.shape
    n = w_main.shape[1]
    grid = (m // tm, n // tn)
    return pl.pallas_call(
        functools.partial(_inproj_kernel, row_chunk=256),
        grid=grid,
        in_specs=[
            pl.BlockSpec((tm, d), lambda i, j: (i, 0)),
            pl.BlockSpec((1, d), lambda i, j: (0, 0)),
            pl.BlockSpec((d, tn), lambda i, j: (0, j)),
            pl.BlockSpec((d, LANES), lambda i, j: (0, 0)),
            pl.BlockSpec((1, LANES), lambda i, j: (0, 0)),
        ],
        out_specs=[
            pl.BlockSpec((tm, tn), lambda i, j: (i, j)),
            pl.BlockSpec((tm, LANES), lambda i, j: (i, 0)),
        ],
        out_shape=[
            jax.ShapeDtypeStruct((m, n), BF16),
            jax.ShapeDtypeStruct((m, LANES), F32),
        ],
        scratch_shapes=[pltpu.VMEM((tm, d), BF16)],
        compiler_params=_compiler_params(("parallel", "arbitrary"), 48),
        name="in_proj",
    )(x, norm_w, w_main, w_dt, dt_bias)


def _half_rms(x, w):
    lane = lax.broadcasted_iota(jnp.int32, x.shape, 1)
    left = lane < ATTN_HEAD_DIM
    x2 = x * x
    s_left = jnp.sum(jnp.where(left, x2, 0.0), axis=-1, keepdims=True)
    s_right = jnp.sum(jnp.where(left, 0.0, x2), axis=-1, keepdims=True)
    ms = jnp.where(left, s_left, s_right) * (1.0 / ATTN_HEAD_DIM)
    return x * lax.rsqrt(ms + EPS) * w


def _qkv_prep_kernel(q_ref, k_ref, v_ref, qw_ref, kw_ref, qn_ref, kn_ref, vt_ref, *, q_scale):
    qn = _half_rms(q_ref[...].astype(F32), qw_ref[...]) * q_scale
    qn_ref[...] = qn.astype(qn_ref.dtype)
    kn_ref[...] = _half_rms(k_ref[...].astype(F32), kw_ref[...]).astype(kn_ref.dtype)
    vt_ref[0, 0, 0] = v_ref[...].astype(F32).T.astype(vt_ref.dtype)


def _qkv_prep(proj, qw, kw, *, batch, seq, heads, ts):
    m = proj.shape[0]
    ns = seq // ts
    width = heads * ATTN_V_DIM
    q_scale = ATTN_HEAD_DIM ** -0.5 * LOG2E
    row = lambda b, h, s: b * ns + s
    return pl.pallas_call(
        functools.partial(_qkv_prep_kernel, q_scale=q_scale),
        grid=(batch, heads, ns),
        in_specs=[
            pl.BlockSpec((ts, ATTN_V_DIM), lambda b, h, s: (row(b, h, s), h)),
            pl.BlockSpec((ts, ATTN_V_DIM), lambda b, h, s: (row(b, h, s), heads + h)),
            pl.BlockSpec((ts, ATTN_V_DIM), lambda b, h, s: (row(b, h, s), 2 * heads + h)),
            pl.BlockSpec((1, ATTN_V_DIM), lambda b, h, s: (0, 0)),
            pl.BlockSpec((1, ATTN_V_DIM), lambda b, h, s: (0, 0)),
        ],
        out_specs=[
            pl.BlockSpec((ts, ATTN_V_DIM), lambda b, h, s: (row(b, h, s), h)),
            pl.BlockSpec((ts, ATTN_V_DIM), lambda b, h, s: (row(b, h, s), h)),
            pl.BlockSpec((1, 1, 1, ATTN_V_DIM, ts), lambda b, h, s: (b, h, s, 0, 0)),
        ],
        out_shape=[
            jax.ShapeDtypeStruct((m, width), BF16),
            jax.ShapeDtypeStruct((m, width), BF16),
            jax.ShapeDtypeStruct((batch, heads, ns, ATTN_V_DIM, ts), BF16),
        ],
        compiler_params=_compiler_params(("parallel", "parallel", "parallel"), 32),
        name="qkv_prep",
    )(proj, proj, proj, qw, kw)


def _attn_kernel(lam_ref, q_ref, k_ref, vt_ref, z_ref, sw_ref, o_ref,
                 qm_ref, s_ref, mt_ref, m_ref, l_ref, acc_ref, *, lambda_init):
    qi = pl.program_id(2)
    q = q_ref[...]
    lane = lax.broadcasted_iota(jnp.int32, q.shape, 1)
    zero = jnp.zeros_like(q)
    qm_ref[0] = jnp.where(lane < ATTN_HEAD_DIM, q, zero)
    qm_ref[1] = jnp.where(lane < ATTN_HEAD_DIM, zero, q)

    m_ref[...] = jnp.full(m_ref.shape, NEG_BIG, F32)
    l_ref[...] = jnp.zeros(l_ref.shape, F32)
    acc_ref[...] = jnp.zeros(acc_ref.shape, F32)

    def scores(kj, slot, masked):
        k = k_ref[0, kj]
        for mp in range(2):
            s = lax.dot_general(k, qm_ref[mp], NT_DIMS, preferred_element_type=F32)
            if masked:
                key = lax.broadcasted_iota(jnp.int32, s.shape, 0)
                qry = lax.broadcasted_iota(jnp.int32, s.shape, 1)
                s = jnp.where(key <= qry, s, NEG_BIG)
            s_ref[slot, mp] = s
            mt_ref[slot, mp] = jnp.max(s, axis=0, keepdims=True)

    def softmax_pv(kj, slot):
        vt = vt_ref[0, 0, kj]
        for mp in range(2):
            m_old = m_ref[mp]
            m_new = jnp.maximum(m_old, mt_ref[slot, mp])
            p = jnp.exp2(s_ref[slot, mp] - m_new)
            alpha = jnp.exp2(m_old - m_new)
            l_ref[mp] = alpha * l_ref[mp] + jnp.sum(p, axis=0, keepdims=True)
            acc_ref[mp] = alpha * acc_ref[mp] + jnp.dot(vt, p.astype(BF16), preferred_element_type=F32)
            m_ref[mp] = m_new

    def kv_tile(step):
        return jnp.where(step == 0, qi, step - 1)

    scores(qi, 0, True)

    def pair(p, carry):
        scores(2 * p, 1, False)
        softmax_pv(kv_tile(2 * p), 0)
        scores(2 * p + 1, 0, False)
        softmax_pv(kv_tile(2 * p + 1), 1)
        return carry

    lax.fori_loop(0, qi // 2, pair, 0)

    @pl.when(qi % 2 == 1)
    def _():
        scores(qi - 1, 1, False)
        softmax_pv(kv_tile(qi - 1), 0)
        softmax_pv(kv_tile(qi), 1)

    @pl.when(qi % 2 == 0)
    def _():
        softmax_pv(kv_tile(qi), 0)

    lf = lam_ref[...]
    s01 = jnp.sum(lf[0:1] * lf[1:2], axis=-1, keepdims=True)
    s23 = jnp.sum(lf[2:3] * lf[3:4], axis=-1, keepdims=True)
    lam = jnp.exp(s01) - jnp.exp(s23) + lambda_init
    o = acc_ref[0] / l_ref[0] - lam * (acc_ref[1] / l_ref[1])
    ms = jnp.mean(o * o, axis=0, keepdims=True)
    on = (o * lax.rsqrt(ms + EPS)).T
    z = z_ref[...].astype(F32)
    y = on * sw_ref[...] * (1.0 - lambda_init) * _silu(z)
    o_ref[...] = y.astype(o_ref.dtype)


def _diff_attn(diff_lambda, qn, kn, vt, proj, subln_w, *, batch, seq, heads, tq, lambda_init, z_col0):
    m, width = qn.shape
    nq = seq // tq
    kn4 = kn.reshape(batch, nq, tq, width)
    row = lambda b, h, i: b * nq + i
    return pl.pallas_call(
        functools.partial(_attn_kernel, lambda_init=lambda_init),
        grid=(batch, heads, nq),
        in_specs=[
            pl.BlockSpec(diff_lambda.shape, lambda b, h, i: (0, 0)),
            pl.BlockSpec((tq, ATTN_V_DIM), lambda b, h, i: (row(b, h, i), h)),
            pl.BlockSpec((1, nq, tq, ATTN_V_DIM), lambda b, h, i: (b, 0, 0, h)),
            pl.BlockSpec((1, 1, nq, ATTN_V_DIM, tq), lambda b, h, i: (b, h, 0, 0, 0)),
            pl.BlockSpec((tq, ATTN_V_DIM), lambda b, h, i: (row(b, h, i), z_col0 + h)),
            pl.BlockSpec((1, ATTN_V_DIM), lambda b, h, i: (0, 0)),
        ],
        out_specs=pl.BlockSpec((tq, ATTN_V_DIM), lambda b, h, i: (row(b, h, i), h)),
        out_shape=jax.ShapeDtypeStruct((m, width), BF16),
        scratch_shapes=[
            pltpu.VMEM((2, tq, ATTN_V_DIM), BF16),
            pltpu.VMEM((2, 2, tq, tq), F32),
            pltpu.VMEM((2, 2, 1, tq), F32),
            pltpu.VMEM((2, 1, tq), F32),
            pltpu.VMEM((2, 1, tq), F32),
            pltpu.VMEM((2, ATTN_V_DIM, tq), F32),
        ],
        compiler_params=_compiler_params(("parallel", "parallel", "arbitrary"), 48),
        name="diff_attn",
    )(diff_lambda, qn, kn4, vt, proj, subln_w)


def _causal_conv_silu(cur_ref, hist_ref, ext_ref, w_ref, b_ref):
    rows = cur_ref.shape[0]
    cur = cur_ref[...].astype(F32)
    ext_ref[0:CONV_HIST, :] = hist_ref[...]
    ext_ref[CONV_HIST:CONV_HIST + rows, :] = cur
    hist_ref[...] = cur[rows - CONV_HIST:rows, :]
    acc = b_ref[...] + w_ref[SSD_CONV - 1:SSD_CONV, :] * cur
    for k in range(SSD_CONV - 1):
        start = CONV_HIST - (SSD_CONV - 1) + k
        acc = acc + w_ref[k:k + 1, :] * ext_ref[start:start + rows, :]
    return _silu(acc)


def _split3(x):
    hi = x.astype(BF16)
    r1 = x - hi.astype(F32)
    mid = r1.astype(BF16)
    lo = (r1 - mid.astype(F32)).astype(BF16)
    return hi, mid, lo


def _ssd_kernel(xs_ref, b_ref, c_ref, zs_ref, dt_ref,
                cwx_ref, cbx_ref, cwb_ref, cbb_ref, cwc_ref, cbc_ref,
                alog_ref, dskip_ref, nw_ref,
                y_ref,
                hx_ref, hb_ref, hc_ref, ex_ref, eb_ref, ec_ref, state_ref):
    c = pl.program_id(1)
    L = xs_ref.shape[0]
    N = SSD_STATE
    pair_w = 2 * SSD_HEAD_DIM
    n_pairs = xs_ref.shape[1] // pair_w
    pairs_per_group = n_pairs // SSD_GROUPS

    @pl.when(c == 0)
    def _():
        hx_ref[...] = jnp.zeros(hx_ref.shape, F32)
        hb_ref[...] = jnp.zeros(hb_ref.shape, F32)
        hc_ref[...] = jnp.zeros(hc_ref.shape, F32)
        state_ref[...] = jnp.zeros(state_ref.shape, F32)

    xs = _causal_conv_silu(xs_ref, hx_ref, ex_ref, cwx_ref, cbx_ref)
    bm = _causal_conv_silu(b_ref, hb_ref, eb_ref, cwb_ref, cbb_ref)
    cm = _causal_conv_silu(c_ref, hc_ref, ec_ref, cwc_ref, cbc_ref)

    dt = dt_ref[...]
    a_neg = -jnp.exp(alog_ref[...])
    dta = dt * a_neg
    row = lax.broadcasted_iota(jnp.int32, (L, L), 0)
    col = lax.broadcasted_iota(jnp.int32, (L, L), 1)
    causal = row >= col
    tri = jnp.where(causal, 1.0, 0.0).astype(BF16)
    a_cum = sum(jnp.dot(tri, part, preferred_element_type=F32) for part in _split3(dta))
    a_cum_t = a_cum.T
    dt_t = dt.T

    xs_bf = xs.astype(BF16)
    lane = lax.broadcasted_iota(jnp.int32, (L, pair_w), 1)
    left = lane < SSD_HEAD_DIM

    y_slabs = []
    for g in range(SSD_GROUPS):
        bg = bm[:, g * N:(g + 1) * N]
        cg = cm[:, g * N:(g + 1) * N]
        cb = lax.dot_general(cg.astype(BF16), bg.astype(BF16), NT_DIMS, preferred_element_type=F32)
        bg_t = bg.T
        for jp in range(pairs_per_group):
            pr = g * pairs_per_group + jp
            lanes = slice(pr * pair_w, (pr + 1) * pair_w)
            x_pair = xs_bf[:, lanes]
            s_pair = state_ref[:, lanes]
            lhs_y, lhs_s, chunk_decay = [], [], []
            for e in range(2):
                h = 2 * pr + e
                colb = jnp.broadcast_to(a_cum[:, h:h + 1], (L, L))
                rowb = a_cum_t[h:h + 1, :]
                dtrow = dt_t[h:h + 1, :]
                decay = jnp.exp(jnp.where(causal, colb - rowb, NEG_BIG))
                m_h = cb * decay * dtrow
                cw_h = cg * jnp.exp(colb)
                lhs_y.append(jnp.concatenate([m_h, cw_h], axis=1).astype(BF16))
                a_last = colb[L - 1:L, :]
                wrow = dtrow * jnp.exp(a_last - rowb)
                lhs_s.append((bg_t * wrow).astype(BF16))
                chunk_decay.append(jnp.exp(a_last))
            rhs_y = jnp.concatenate([x_pair, s_pair.astype(BF16)], axis=0)
            y2 = jnp.dot(jnp.concatenate(lhs_y, axis=0), rhs_y, preferred_element_type=F32)
            y_slabs.append(jnp.where(left, y2[:L], y2[L:]))
            s2 = jnp.dot(jnp.concatenate(lhs_s, axis=0), x_pair, preferred_element_type=F32)
            state_ref[:, lanes] = (jnp.where(left, s2[:N], s2[N:])
                                   + s_pair * jnp.where(left, chunk_decay[0], chunk_decay[1]))

    y = jnp.concatenate(y_slabs, axis=1) + xs * dskip_ref[...]
    gated = y * _silu(zs_ref[...].astype(F32))
    gw = gated.shape[1] // SSD_GROUPS
    for g in range(SSD_GROUPS):
        blk = gated[:, g * gw:(g + 1) * gw]
        ms = jnp.mean(blk * blk, axis=-1, keepdims=True)
        y_ref[:, g * gw:(g + 1) * gw] = (blk * lax.rsqrt(ms + EPS) * nw_ref[:, g * gw:(g + 1) * gw]).astype(y_ref.dtype)


def _ssd(proj, dt, conv_wx, conv_bx, conv_wb, conv_bb, conv_wc, conv_bc, a_log, d_skip, norm_w,
         *, batch, seq, xs_col, b_col, c_col, zs_col):
    m = proj.shape[0]
    L = SSD_CHUNK
    nc = seq // L
    width = conv_wx.shape[1]
    gn = conv_wb.shape[1]
    row = lambda b, c: b * nc + c
    const = lambda b, c: (0, 0)
    return pl.pallas_call(
        _ssd_kernel,
        grid=(batch, nc),
        in_specs=[
            pl.BlockSpec((L, width), lambda b, c: (row(b, c), xs_col)),
            pl.BlockSpec((L, gn), lambda b, c: (row(b, c), b_col)),
            pl.BlockSpec((L, gn), lambda b, c: (row(b, c), c_col)),
            pl.BlockSpec((L, width), lambda b, c: (row(b, c), zs_col)),
            pl.BlockSpec((L, LANES), lambda b, c: (row(b, c), 0)),
            pl.BlockSpec(conv_wx.shape, const), pl.BlockSpec(conv_bx.shape, const),
            pl.BlockSpec(conv_wb.shape, const), pl.BlockSpec(conv_bb.shape, const),
            pl.BlockSpec(conv_wc.shape, const), pl.BlockSpec(conv_bc.shape, const),
            pl.BlockSpec(a_log.shape, const), pl.BlockSpec(d_skip.shape, const),
            pl.BlockSpec(norm_w.shape, const),
        ],
        out_specs=pl.BlockSpec((L, width), lambda b, c: (row(b, c), 0)),
        out_shape=jax.ShapeDtypeStruct((m, width), BF16),
        scratch_shapes=[
            pltpu.VMEM((CONV_HIST, width), F32), pltpu.VMEM((CONV_HIST, gn), F32), pltpu.VMEM((CONV_HIST, gn), F32),
            pltpu.VMEM((CONV_HIST + L, width), F32), pltpu.VMEM((CONV_HIST + L, gn), F32),
            pltpu.VMEM((CONV_HIST + L, gn), F32),
            pltpu.VMEM((SSD_STATE, width), F32),
        ],
        compiler_params=_compiler_params(("parallel", "arbitrary"), 48),
        name="ssd",
    )(proj, proj, proj, proj, dt, conv_wx, conv_bx, conv_wb, conv_bb, conv_wc, conv_bc, a_log, d_skip, norm_w)


def _outproj_kernel(ya_ref, ys_ref, ga_ref, gs_ref, x_ref, wpa_ref, wps_ref, wo_ref, o_ref):
    pa = jnp.dot(ya_ref[...], wpa_ref[...], preferred_element_type=F32)
    ps = jnp.dot(ys_ref[...], wps_ref[...], preferred_element_type=F32)
    merged = (jax.nn.sigmoid(ga_ref[...].astype(F32)) * pa
              + jax.nn.sigmoid(gs_ref[...].astype(F32)) * ps)
    o_ref[...] = x_ref[...] + jnp.dot(merged.astype(BF16), wo_ref[...], preferred_element_type=F32)


def _out_proj(y_a, y_s, proj, x, w_pa, w_ps, w_o, *, tm, ga_col, gs_col):
    m, d = x.shape
    const = lambda i: (0, 0)
    return pl.pallas_call(
        _outproj_kernel,
        grid=(m // tm,),
        in_specs=[
            pl.BlockSpec((tm, y_a.shape[1]), lambda i: (i, 0)),
            pl.BlockSpec((tm, y_s.shape[1]), lambda i: (i, 0)),
            pl.BlockSpec((tm, d), lambda i: (i, ga_col)),
            pl.BlockSpec((tm, d), lambda i: (i, gs_col)),
            pl.BlockSpec((tm, d), lambda i: (i, 0)),
            pl.BlockSpec(w_pa.shape, const),
            pl.BlockSpec(w_ps.shape, const),
            pl.BlockSpec(w_o.shape, const),
        ],
        out_specs=pl.BlockSpec((tm, d), lambda i: (i, 0)),
        out_shape=jax.ShapeDtypeStruct((m, d), F32),
        compiler_params=_compiler_params(("parallel",), 48),
        name="out_proj",
    )(y_a, y_s, proj, proj, x, w_pa, w_ps, w_o)


def _lambda_init(layer_idx):
    return 0.8 - 0.6 * math.exp(-0.3 * layer_idx)


def kernel(x, norm_w, w_in, q_norm_w, k_norm_w, diff_lambda, subln_w, conv_w, conv_b, dt_bias, a_log, d_skip,
           ssd_norm_w, w_proj_attn, w_proj_ssd, w_out):
    batch, seq, d_model = x.shape
    depth = w_in.shape[0]
    attn_w = w_proj_attn.shape[1]
    ssd_w = w_proj_ssd.shape[1]
    ssd_heads = dt_bias.shape[1]
    heads = attn_w // ATTN_V_DIM
    gn = SSD_GROUPS * SSD_STATE

    o_za = 3 * attn_w
    o_xs = 4 * attn_w
    o_b = o_xs + ssd_w
    o_c = o_b + gn
    o_zs = o_c + gn
    o_dt = o_zs + ssd_w
    o_gate = o_dt + ssd_heads
    w_main = jnp.concatenate([w_in[:, :, :o_b], w_in[:, :, o_zs:o_dt], w_in[:, :, o_gate:],
                              w_in[:, :, o_b:o_zs]], axis=-1).astype(BF16)
    w_dt = jnp.pad(w_in[:, :, o_dt:o_gate], ((0, 0), (0, 0), (0, LANES - ssd_heads))).astype(BF16)
    pad_h = ((0, 0), (0, LANES - ssd_heads))
    dt_bias_p = jnp.pad(dt_bias, pad_h)[:, None, :]
    a_log_p = jnp.pad(a_log, pad_h)[:, None, :]
    d_skip_x = jnp.repeat(d_skip, SSD_HEAD_DIM, axis=-1)[:, None, :]
    qw = jnp.tile(q_norm_w, (1, 2))[:, None, :]
    kw = jnp.tile(k_norm_w, (1, 2))[:, None, :]
    w_pa = w_proj_attn.astype(BF16)
    w_ps = w_proj_ssd.astype(BF16)
    w_o = w_out.astype(BF16)

    c_xs = o_xs // ssd_w
    c_zs = (o_xs + ssd_w) // ssd_w
    c_ga = (o_xs + 2 * ssd_w) // d_model
    c_gs = c_ga + 1
    c_b = (o_xs + 2 * ssd_w + 2 * d_model) // gn
    c_c = c_b + 1
    c_za = o_za // ATTN_V_DIM

    tq = min(512, seq)
    xf = x.reshape(batch * seq, d_model)
    for l in range(depth):
        proj, dt = _in_proj(xf, norm_w[l][None, :], w_main[l], w_dt[l], dt_bias_p[l], tm=1024, tn=1024)
        qn, kn, vt = _qkv_prep(proj, qw[l], kw[l], batch=batch, seq=seq, heads=heads, ts=tq)
        y_a = _diff_attn(diff_lambda[l], qn, kn, vt, proj, subln_w[l][None, :], batch=batch, seq=seq, heads=heads,
                         tq=tq, lambda_init=_lambda_init(l), z_col0=c_za)
        y_s = _ssd(proj, dt, conv_w[l][:, :ssd_w], conv_b[l][None, :ssd_w],
                   conv_w[l][:, ssd_w:ssd_w + gn], conv_b[l][None, ssd_w:ssd_w + gn],
                   conv_w[l][:, ssd_w + gn:], conv_b[l][None, ssd_w + gn:],
                   a_log_p[l], d_skip_x[l], ssd_norm_w[l][None, :],
                   batch=batch, seq=seq, xs_col=c_xs, b_col=c_b, c_col=c_c, zs_col=c_zs)
        xf = _out_proj(y_a, y_s, proj, xf, w_pa[l], w_ps[l], w_o[l], tm=512, ga_col=c_ga, gs_col=c_gs)
    return xf.reshape(batch, seq, d_model)
```

```python
import functools
import math

import jax
import jax.numpy as jnp
from jax import lax
from jax.experimental import pallas as pl
from jax.experimental.pallas import tpu as pltpu

F32 = jnp.float32
BF16 = jnp.bfloat16

EPS = 1e-6
LANES = 128
ATTN_HEAD_DIM = 64
ATTN_V_DIM = 2 * ATTN_HEAD_DIM
SSD_HEAD_DIM = 64
SSD_GROUPS = 4
SSD_STATE = 128
SSD_CHUNK = 128
SSD_CONV = 4
NEG_BIG = -1e30
LOG2E = math.log2(math.e)

NT_DIMS = (((1,), (1,)), ((), ()))

IN_PROJ_TM = 1024
IN_PROJ_TN = 2816
ATTN_TILE = 512
OUT_PROJ_TM = 512
VMEM_LIMIT_MIB = 48


def _compiler_params(semantics):
    return pltpu.CompilerParams(dimension_semantics=semantics,
                                vmem_limit_bytes=VMEM_LIMIT_MIB * 1024 * 1024)


def _silu(x):
    h = 0.5 * x
    return h + h * jnp.tanh(h)


def _inproj_kernel(x_ref, nw_ref, w_ref, wdt_ref, dtb_ref, proj_ref, dt_ref, h_ref, *, row_chunk):
    j = pl.program_id(1)

    @pl.when(j == 0)
    def _():
        tm = x_ref.shape[0]
        for r in range(tm // row_chunk):
            rows = pl.ds(r * row_chunk, row_chunk)
            x = x_ref[rows, :]
            ms = jnp.mean(x * x, axis=-1, keepdims=True)
            h = (x * lax.rsqrt(ms + EPS) * nw_ref[...]).astype(BF16)
            h_ref[rows, :] = h
            raw = jnp.dot(h, wdt_ref[...], preferred_element_type=F32) + dtb_ref[...]
            dt_ref[rows, :] = jnp.maximum(raw, 0.0) + jnp.log1p(jnp.exp(-jnp.abs(raw)))

    proj_ref[...] = jnp.dot(h_ref[...], w_ref[...], preferred_element_type=F32).astype(proj_ref.dtype)


def _in_proj(x, norm_w, w_main, w_dt, dt_bias, *, tm, tn):
    m, d = x.shape
    n = w_main.shape[1]
    grid = (m // tm, n // tn)
    return pl.pallas_call(
        functools.partial(_inproj_kernel, row_chunk=min(256, tm)),
        grid=grid,
        in_specs=[
            pl.BlockSpec((tm, d), lambda i, j: (i, 0)),
            pl.BlockSpec((1, d), lambda i, j: (0, 0)),
            pl.BlockSpec((d, tn), lambda i, j: (0, j)),
            pl.BlockSpec((d, LANES), lambda i, j: (0, 0)),
            pl.BlockSpec((1, LANES), lambda i, j: (0, 0)),
        ],
        out_specs=[
            pl.BlockSpec((tm, tn), lambda i, j: (i, j)),
            pl.BlockSpec((tm, LANES), lambda i, j: (i, 0)),
        ],
        out_shape=[
            jax.ShapeDtypeStruct((m, n), BF16),
            jax.ShapeDtypeStruct((m, LANES), F32),
        ],
        scratch_shapes=[pltpu.VMEM((tm, d), BF16)],
        compiler_params=_compiler_params(("parallel", "arbitrary")),
        name="in_proj",
    )(x, norm_w, w_main, w_dt, dt_bias)


def _half_rms(x, w):
    lane = lax.broadcasted_iota(jnp.int32, x.shape, 1)
    left = lane < ATTN_HEAD_DIM
    x2 = x * x
    s_left = jnp.sum(jnp.where(left, x2, 0.0), axis=-1, keepdims=True)
    s_right = jnp.sum(jnp.where(left, 0.0, x2), axis=-1, keepdims=True)
    ms = jnp.where(left, s_left, s_right) * (1.0 / ATTN_HEAD_DIM)
    return x * lax.rsqrt(ms + EPS) * w


def _attn_kernel(sched_ref, lam_ref, li_ref, q_ref, k_ref, v_ref, z_ref, qw_ref, kw_ref, sw_ref, o_ref,
                 kn_ref, vt_ref, qm_ref, s_ref, mt_ref, m_ref, l_ref, acc_ref, *, tile, q_scale):
    n_tiles = q_ref.shape[0] // tile
    n_full = n_tiles * (n_tiles - 1) // 2
    rows = lambda i: slice(i * tile, (i + 1) * tile)

    def prep(j):
        kn_ref[j] = _half_rms(k_ref[rows(j), :].astype(F32), kw_ref[...]).astype(BF16)
        vt_ref[j] = v_ref[rows(j), :].astype(F32).T.astype(BF16)
        q = (_half_rms(q_ref[rows(j), :].astype(F32), qw_ref[...]) * q_scale).astype(BF16)
        lane = lax.broadcasted_iota(jnp.int32, q.shape, 1)
        zero = jnp.zeros_like(q)
        qm_ref[j, 0] = jnp.where(lane < ATTN_HEAD_DIM, q, zero)
        qm_ref[j, 1] = jnp.where(lane < ATTN_HEAD_DIM, zero, q)
        m_ref[j] = jnp.full(m_ref.shape[1:], NEG_BIG, F32)
        l_ref[j] = jnp.zeros(l_ref.shape[1:], F32)
        acc_ref[j] = jnp.zeros(acc_ref.shape[1:], F32)

    def scores(qi, kj, slot, diag):
        k = kn_ref[kj]
        for mp in range(2):
            s = lax.dot_general(k, qm_ref[qi, mp], NT_DIMS, preferred_element_type=F32)
            if diag:
                key = lax.broadcasted_iota(jnp.int32, s.shape, 0)
                qry = lax.broadcasted_iota(jnp.int32, s.shape, 1)
                s = jnp.where(key <= qry, s, NEG_BIG)
            s_ref[slot, mp] = s
            mt_ref[slot, mp] = jnp.max(s, axis=0, keepdims=True)

    def softmax_pv(qi, kj, slot):
        vt = vt_ref[kj]
        for mp in range(2):
            m_old = m_ref[qi, mp]
            m_new = jnp.maximum(m_old, mt_ref[slot, mp])
            p = jnp.exp2(s_ref[slot, mp] - m_new)
            alpha = jnp.exp2(m_old - m_new)
            l_ref[qi, mp] = alpha * l_ref[qi, mp] + jnp.sum(p, axis=0, keepdims=True)
            acc_ref[qi, mp] = alpha * acc_ref[qi, mp] + jnp.dot(vt, p.astype(BF16), preferred_element_type=F32)
            m_ref[qi, mp] = m_new

    def full_step(t):
        return sched_ref[0, t], sched_ref[1, t]

    for j in range(n_tiles + 1):
        if j < n_tiles:
            prep(j)
            scores(j, j, j % 2, True)
        if j >= 1:
            softmax_pv(j - 1, j - 1, (j - 1) % 2)

    if n_full >= 2:
        scores(*full_step(0), 0, False)

        def pair(p, carry):
            scores(*full_step(2 * p + 1), 1, False)
            softmax_pv(*full_step(2 * p), 0)
            scores(*full_step(2 * p + 2), 0, False)
            softmax_pv(*full_step(2 * p + 1), 1)
            return carry

        lax.fori_loop(0, n_full // 2 - 1, pair, 0)
        scores(*full_step(n_full - 1), 1, False)
        softmax_pv(*full_step(n_full - 2), 0)
        softmax_pv(*full_step(n_full - 1), 1)
    elif n_full == 1:
        scores(*full_step(0), 0, False)
        softmax_pv(*full_step(0), 0)

    lf = lam_ref[...]
    li = li_ref[:, 0:1]
    s01 = jnp.sum(lf[0:1] * lf[1:2], axis=-1, keepdims=True)
    s23 = jnp.sum(lf[2:3] * lf[3:4], axis=-1, keepdims=True)
    lam = jnp.exp(s01) - jnp.exp(s23) + li
    for j in range(n_tiles):
        o = acc_ref[j, 0] / l_ref[j, 0] - lam * (acc_ref[j, 1] / l_ref[j, 1])
        ms = jnp.mean(o * o, axis=0, keepdims=True)
        on = (o * lax.rsqrt(ms + EPS)).T
        z = z_ref[rows(j), :].astype(F32)
        y = on * sw_ref[...] * (1.0 - li) * _silu(z)
        o_ref[rows(j), :] = y.astype(o_ref.dtype)


def _diff_attn(diff_lambda, lam_init, proj, qw, kw, subln_w, *, batch, seq, heads, tile, q_col0, k_col0, v_col0,
               z_col0):
    m = proj.shape[0]
    n_tiles = seq // tile
    assert n_tiles % 2 == 0 or n_tiles == 1, "phase B pairs the below-diagonal tiles"
    q_scale = ATTN_HEAD_DIM ** -0.5 * LOG2E
    pairs = [(qi, kj) for qi in range(n_tiles) for kj in range(qi)] or [(0, 0)]
    sched = jnp.asarray(pairs, jnp.int32).T
    const = lambda b, h, s: (0, 0)
    col = lambda c0: (lambda b, h, s: (b, c0 + h))
    blk = (seq, ATTN_V_DIM)
    grid_spec = pltpu.PrefetchScalarGridSpec(
        num_scalar_prefetch=1,
        grid=(batch, heads),
        in_specs=[
            pl.BlockSpec(diff_lambda.shape, const),
            pl.BlockSpec(lam_init.shape, const),
            pl.BlockSpec(blk, col(q_col0)),
            pl.BlockSpec(blk, col(k_col0)),
            pl.BlockSpec(blk, col(v_col0)),
            pl.BlockSpec(blk, col(z_col0)),
            pl.BlockSpec(qw.shape, const),
            pl.BlockSpec(kw.shape, const),
            pl.BlockSpec(subln_w.shape, const),
        ],
        out_specs=pl.BlockSpec(blk, lambda b, h, s: (b, h)),
        scratch_shapes=[
            pltpu.VMEM((n_tiles, tile, ATTN_V_DIM), BF16),
            pltpu.VMEM((n_tiles, ATTN_V_DIM, tile), BF16),
            pltpu.VMEM((n_tiles, 2, tile, ATTN_V_DIM), BF16),
            pltpu.VMEM((2, 2, tile, tile), F32),
            pltpu.VMEM((2, 2, 1, tile), F32),
            pltpu.VMEM((n_tiles, 2, 1, tile), F32),
            pltpu.VMEM((n_tiles, 2, 1, tile), F32),
            pltpu.VMEM((n_tiles, 2, ATTN_V_DIM, tile), F32),
        ],
    )
    return pl.pallas_call(
        functools.partial(_attn_kernel, tile=tile, q_scale=q_scale),
        grid_spec=grid_spec,
        out_shape=jax.ShapeDtypeStruct((m, heads * ATTN_V_DIM), BF16),
        compiler_params=_compiler_params(("parallel", "parallel")),
        name="diff_attn",
    )(sched, diff_lambda, lam_init, proj, proj, proj, proj, qw, kw, subln_w)


def _causal_conv_silu(cur_ref, prev_ref, shift, w_ref, b_ref):
    rows = cur_ref.shape[0]
    cur = cur_ref[...]
    ext = jnp.concatenate([prev_ref[...], cur], axis=0)
    prev_ref[...] = cur
    shifted = jnp.dot(shift, ext, preferred_element_type=F32)
    acc = b_ref[...] + w_ref[SSD_CONV - 1:SSD_CONV, :] * cur.astype(F32)
    for k in range(SSD_CONV - 1):
        acc = acc + w_ref[k:k + 1, :] * shifted[k * rows:(k + 1) * rows, :]
    return _silu(acc)


def _split3(x):
    hi = x.astype(BF16)
    r1 = x - hi.astype(F32)
    mid = r1.astype(BF16)
    lo = (r1 - mid.astype(F32)).astype(BF16)
    return hi, mid, lo


def _ssd_kernel(xbc_ref, zs_ref, dt_ref, cw_ref, cb_ref, alog_ref, dskip_ref, nw_ref, shift_ref,
                y_ref,
                prev_ref, state_ref):
    c = pl.program_id(1)
    L = xbc_ref.shape[0]
    N = SSD_STATE
    width = y_ref.shape[1]
    gn = SSD_GROUPS * N
    pair_w = 2 * SSD_HEAD_DIM
    n_pairs = width // pair_w
    pairs_per_group = n_pairs // SSD_GROUPS

    @pl.when(c == 0)
    def _():
        prev_ref[...] = jnp.zeros(prev_ref.shape, prev_ref.dtype)
        state_ref[...] = jnp.zeros(state_ref.shape, F32)

    xbc = _causal_conv_silu(xbc_ref, prev_ref, shift_ref[...], cw_ref, cb_ref)
    xs = xbc[:, :width]
    bm = xbc[:, width:width + gn]
    cm = xbc[:, width + gn:]

    dt = dt_ref[...]
    a_neg = -jnp.exp(alog_ref[...]) * LOG2E
    dta = dt * a_neg
    row = lax.broadcasted_iota(jnp.int32, (L, L), 0)
    col = lax.broadcasted_iota(jnp.int32, (L, L), 1)
    causal = row >= col
    tri = jnp.where(causal, 1.0, 0.0).astype(BF16)
    a_cum = sum(jnp.dot(tri, part, preferred_element_type=F32) for part in _split3(dta))
    a_cum_t = a_cum.T
    dt_t = dt.T

    xs_bf = xs.astype(BF16)
    lane = lax.broadcasted_iota(jnp.int32, (L, pair_w), 1)
    left = lane < SSD_HEAD_DIM

    y_slabs = []
    for g in range(SSD_GROUPS):
        bg = bm[:, g * N:(g + 1) * N]
        cg = cm[:, g * N:(g + 1) * N]
        cb = lax.dot_general(cg.astype(BF16), bg.astype(BF16), NT_DIMS, preferred_element_type=F32)
        cb = jnp.where(causal, cb, 0.0)
        bg_t = bg.T
        for jp in range(pairs_per_group):
            pr = g * pairs_per_group + jp
            lanes = slice(pr * pair_w, (pr + 1) * pair_w)
            x_pair = xs_bf[:, lanes]
            s_pair = state_ref[:, lanes]
            lhs_y, lhs_s, chunk_decay = [], [], []
            for e in range(2):
                h = 2 * pr + e
                colb = jnp.broadcast_to(a_cum[:, h:h + 1], (L, L))
                rowb = a_cum_t[h:h + 1, :]
                dtrow = dt_t[h:h + 1, :]
                decay = jnp.exp2(jnp.minimum(colb - rowb, 0.0))
                m_h = cb * decay * dtrow
                cw_h = cg * jnp.exp2(colb)
                lhs_y.append(jnp.concatenate([m_h, cw_h], axis=1).astype(BF16))
                a_last = colb[L - 1:L, :]
                wrow = dtrow * jnp.exp2(a_last - rowb)
                lhs_s.append((bg_t * wrow).astype(BF16))
                chunk_decay.append(jnp.exp2(a_last))
            rhs_y = jnp.concatenate([x_pair, s_pair.astype(BF16)], axis=0)
            y2 = jnp.dot(jnp.concatenate(lhs_y, axis=0), rhs_y, preferred_element_type=F32)
            y_slabs.append(jnp.where(left, y2[:L], y2[L:]))
            s2 = jnp.dot(jnp.concatenate(lhs_s, axis=0), x_pair, preferred_element_type=F32)
            state_ref[:, lanes] = (jnp.where(left, s2[:N], s2[N:])
                                   + s_pair * jnp.where(left, chunk_decay[0], chunk_decay[1]))

    y = jnp.concatenate(y_slabs, axis=1) + xs * dskip_ref[...]
    gated = y * _silu(zs_ref[...].astype(F32))
    gw = gated.shape[1] // SSD_GROUPS
    for g in range(SSD_GROUPS):
        blk = gated[:, g * gw:(g + 1) * gw]
        ms = jnp.mean(blk * blk, axis=-1, keepdims=True)
        y_ref[:, g * gw:(g + 1) * gw] = (blk * lax.rsqrt(ms + EPS) * nw_ref[:, g * gw:(g + 1) * gw]).astype(y_ref.dtype)


def _ssd(proj, dt, conv_w, conv_b, a_log, d_skip, norm_w, *, batch, seq, xbc_col, zs_col):
    assert SSD_CHUNK == SSD_STATE == LANES, "the chunk kernel reuses (L, L) tiles as (L, N) and head rows as lanes"
    m = proj.shape[0]
    L = SSD_CHUNK
    nc = seq // L
    conv_dim = conv_w.shape[1]
    width = norm_w.shape[1]
    row = lambda b, c: b * nc + c
    const = lambda b, c: (0, 0)
    tap_row = jnp.arange((SSD_CONV - 1) * L)[:, None]
    shift = (jnp.arange(2 * L)[None, :] == tap_row % L + L - (SSD_CONV - 1) + tap_row // L).astype(BF16)
    return pl.pallas_call(
        _ssd_kernel,
        grid=(batch, nc),
        in_specs=[
            pl.BlockSpec((L, conv_dim), lambda b, c: (row(b, c), xbc_col)),
            pl.BlockSpec((L, width), lambda b, c: (row(b, c), zs_col)),
            pl.BlockSpec((L, LANES), lambda b, c: (row(b, c), 0)),
            pl.BlockSpec(conv_w.shape, const), pl.BlockSpec(conv_b.shape, const),
            pl.BlockSpec(a_log.shape, const), pl.BlockSpec(d_skip.shape, const),
            pl.BlockSpec(norm_w.shape, const),
            pl.BlockSpec(shift.shape, const),
        ],
        out_specs=pl.BlockSpec((L, width), lambda b, c: (row(b, c), 0)),
        out_shape=jax.ShapeDtypeStruct((m, width), BF16),
        scratch_shapes=[
            pltpu.VMEM((L, conv_dim), BF16),
            pltpu.VMEM((SSD_STATE, width), F32),
        ],
        compiler_params=_compiler_params(("parallel", "arbitrary")),
        name="ssd",
    )(proj, proj, dt, conv_w, conv_b, a_log, d_skip, norm_w, shift)


def _outproj_kernel(ya_ref, ys_ref, ga_ref, gs_ref, x_ref, wpa_ref, wps_ref, wo_ref, o_ref):
    pa = jnp.dot(ya_ref[...], wpa_ref[...], preferred_element_type=F32)
    ps = jnp.dot(ys_ref[...], wps_ref[...], preferred_element_type=F32)
    merged = (jax.nn.sigmoid(ga_ref[...].astype(F32)) * pa
              + jax.nn.sigmoid(gs_ref[...].astype(F32)) * ps)
    o_ref[...] = x_ref[...] + jnp.dot(merged.astype(BF16), wo_ref[...], preferred_element_type=F32)


def _out_proj(y_a, y_s, proj, x, w_pa, w_ps, w_o, *, tm, ga_col, gs_col):
    m, d = x.shape
    const = lambda i: (0, 0)
    return pl.pallas_call(
        _outproj_kernel,
        grid=(m // tm,),
        in_specs=[
            pl.BlockSpec((tm, y_a.shape[1]), lambda i: (i, 0)),
            pl.BlockSpec((tm, y_s.shape[1]), lambda i: (i, 0)),
            pl.BlockSpec((tm, d), lambda i: (i, ga_col)),
            pl.BlockSpec((tm, d), lambda i: (i, gs_col)),
            pl.BlockSpec((tm, d), lambda i: (i, 0)),
            pl.BlockSpec(w_pa.shape, const),
            pl.BlockSpec(w_ps.shape, const),
            pl.BlockSpec(w_o.shape, const),
        ],
        out_specs=pl.BlockSpec((tm, d), lambda i: (i, 0)),
        out_shape=jax.ShapeDtypeStruct((m, d), F32),
        compiler_params=_compiler_params(("parallel",)),
        name="out_proj",
    )(y_a, y_s, proj, proj, x, w_pa, w_ps, w_o)


def _lambda_init(layer_idx):
    return 0.8 - 0.6 * math.exp(-0.3 * layer_idx)


def kernel(x, norm_w, w_in, q_norm_w, k_norm_w, diff_lambda, subln_w, conv_w, conv_b, dt_bias, a_log, d_skip,
           ssd_norm_w, w_proj_attn, w_proj_ssd, w_out):
    batch, seq, d_model = x.shape
    depth = w_in.shape[0]
    attn_w = w_proj_attn.shape[1]
    ssd_w = w_proj_ssd.shape[1]
    ssd_heads = dt_bias.shape[1]
    heads = attn_w // ATTN_V_DIM
    gn = SSD_GROUPS * SSD_STATE

    o_za = 3 * attn_w
    o_xs = 4 * attn_w
    o_b = o_xs + ssd_w
    o_c = o_b + gn
    o_zs = o_c + gn
    o_dt = o_zs + ssd_w
    o_gate = o_dt + ssd_heads
    w_main = jnp.concatenate([w_in[:, :, :o_xs], w_in[:, :, o_zs:o_dt], w_in[:, :, o_xs:o_zs],
                              w_in[:, :, o_gate:]], axis=-1).astype(BF16)
    w_dt = jnp.pad(w_in[:, :, o_dt:o_gate], ((0, 0), (0, 0), (0, LANES - ssd_heads))).astype(BF16)
    pad_h = ((0, 0), (0, LANES - ssd_heads))
    dt_bias_p = jnp.pad(dt_bias, pad_h)[:, None, :]
    a_log_p = jnp.pad(a_log, pad_h)[:, None, :]
    d_skip_x = jnp.repeat(d_skip, SSD_HEAD_DIM, axis=-1)[:, None, :]
    qw = jnp.tile(q_norm_w, (1, 2))[:, None, :]
    kw = jnp.tile(k_norm_w, (1, 2))[:, None, :]
    w_pa = w_proj_attn.astype(BF16)
    w_ps = w_proj_ssd.astype(BF16)
    w_o = w_out.astype(BF16)

    conv_dim = ssd_w + 2 * gn
    c_zs = o_xs // ssd_w
    c_xbc = (o_xs + ssd_w) // conv_dim
    c_ga = (o_xs + ssd_w + conv_dim) // d_model
    c_gs = c_ga + 1
    c_za = o_za // ATTN_V_DIM
    assert (o_xs + ssd_w) % conv_dim == 0 and o_xs % ssd_w == 0 and (o_xs + ssd_w + conv_dim) % d_model == 0

    rows = batch * seq
    xf = x.reshape(rows, d_model)
    for l in range(depth):
        proj, dt = _in_proj(xf, norm_w[l][None, :], w_main[l], w_dt[l], dt_bias_p[l],
                            tm=min(IN_PROJ_TM, rows), tn=IN_PROJ_TN)
        lam_init = jnp.full((1, LANES), _lambda_init(l), F32)
        y_a = _diff_attn(diff_lambda[l], lam_init, proj, qw[l], kw[l], subln_w[l][None, :], batch=batch, seq=seq,
                         heads=heads, tile=min(ATTN_TILE, seq), q_col0=0, k_col0=heads, v_col0=2 * heads,
                         z_col0=c_za)
        y_s = _ssd(proj, dt, conv_w[l], conv_b[l][None, :], a_log_p[l], d_skip_x[l], ssd_norm_w[l][None, :],
                   batch=batch, seq=seq, xbc_col=c_xbc, zs_col=c_zs)
        xf = _out_proj(y_a, y_s, proj, xf, w_pa[l], w_ps[l], w_o[l], tm=min(OUT_PROJ_TM, rows),
                       ga_col=c_ga, gs_col=c_gs)
    return xf.reshape(batch, seq, d_model)
```

```python
import functools
import math

import jax
import jax.numpy as jnp
from jax import lax
from jax.experimental import pallas as pl
from jax.experimental.pallas import tpu as pltpu

F32 = jnp.float32
BF16 = jnp.bfloat16

EPS = 1e-6
LANES = 128
ATTN_HEAD_DIM = 64
ATTN_V_DIM = 2 * ATTN_HEAD_DIM
SSD_HEAD_DIM = 64
SSD_GROUPS = 4
SSD_STATE = 128
SSD_CHUNK = 128
SSD_CONV = 4
NEG_BIG = -1e30
DECAY_EXP_CLAMP = 64.0
LOG2E = math.log2(math.e)

NT_DIMS = (((1,), (1,)), ((), ()))
PV_EXTRA_ROWS = 16
PHASE_B_UNROLL = 4
BF16_ULP_UP = 2.0 ** -7

IN_PROJ_TM = 1024
IN_PROJ_TN = 2816
ATTN_TILE = 512
OUT_PROJ_TM = 512
SSD_ROWS_PER_STEP = 256
VMEM_LIMIT_MIB = 48


def _compiler_params(semantics):
    return pltpu.CompilerParams(dimension_semantics=semantics,
                                vmem_limit_bytes=VMEM_LIMIT_MIB * 1024 * 1024)


def _silu(x):
    h = 0.5 * x
    return h + h * jnp.tanh(h)


def _inproj_kernel(x_ref, nw_ref, w_ref, wdt_ref, dtb_ref, proj_ref, dt_ref, h_ref, *, row_chunk):
    j = pl.program_id(1)

    @pl.when(j == 0)
    def _():
        tm = x_ref.shape[0]
        for r in range(tm // row_chunk):
            rows = pl.ds(r * row_chunk, row_chunk)
            x = x_ref[rows, :]
            ms = jnp.mean(x * x, axis=-1, keepdims=True)
            h = (x * lax.rsqrt(ms + EPS) * nw_ref[...]).astype(BF16)
            h_ref[rows, :] = h
            raw = jnp.dot(h, wdt_ref[...], preferred_element_type=F32) + dtb_ref[...]
            dt_ref[rows, :] = jnp.maximum(raw, 0.0) + jnp.log1p(jnp.exp(-jnp.abs(raw)))

    proj_ref[...] = jnp.dot(h_ref[...], w_ref[...], preferred_element_type=F32).astype(proj_ref.dtype)


def _in_proj(x, norm_w, w_main, w_dt, dt_bias, *, tm, tn):
    m, d = x.shape
    n = w_main.shape[1]
    grid = (m // tm, n // tn)
    return pl.pallas_call(
        functools.partial(_inproj_kernel, row_chunk=min(256, tm)),
        grid=grid,
        in_specs=[
            pl.BlockSpec((tm, d), lambda i, j: (i, 0)),
            pl.BlockSpec((1, d), lambda i, j: (0, 0)),
            pl.BlockSpec((d, tn), lambda i, j: (0, j)),
            pl.BlockSpec((d, LANES), lambda i, j: (0, 0)),
            pl.BlockSpec((1, LANES), lambda i, j: (0, 0)),
        ],
        out_specs=[
            pl.BlockSpec((tm, tn), lambda i, j: (i, j)),
            pl.BlockSpec((tm, LANES), lambda i, j: (i, 0)),
        ],
        out_shape=[
            jax.ShapeDtypeStruct((m, n), BF16),
            jax.ShapeDtypeStruct((m, LANES), F32),
        ],
        scratch_shapes=[pltpu.VMEM((tm, d), BF16)],
        compiler_params=_compiler_params(("parallel", "arbitrary")),
        name="in_proj",
    )(x, norm_w, w_main, w_dt, dt_bias)


def _half_rms(x, w):
    lane = lax.broadcasted_iota(jnp.int32, x.shape, 1)
    left = lane < ATTN_HEAD_DIM
    x2 = x * x
    s_left = jnp.sum(jnp.where(left, x2, 0.0), axis=-1, keepdims=True)
    s_right = jnp.sum(jnp.where(left, 0.0, x2), axis=-1, keepdims=True)
    ms = jnp.where(left, s_left, s_right) * (1.0 / ATTN_HEAD_DIM)
    return x * lax.rsqrt(ms + EPS) * w


def _full_tile_schedule(n_tiles):
    return [(qi, kj) for qi in range(n_tiles) for kj in range(qi)]


def _attn_kernel(sched_ref, lam_ref, li_ref, q_ref, k_ref, v_ref, z_ref, qw_ref, kw_ref, sw_ref, o_ref,
                 kn_ref, vt_ref, qm_ref, s_ref, mt_ref, m_ref, acc_ref, *, tile, q_scale):
    n_tiles = q_ref.shape[0] // tile
    full_sched = _full_tile_schedule(n_tiles)
    n_full = len(full_sched)
    rows = lambda i: slice(i * tile, (i + 1) * tile)

    def prep(j):
        kn_ref[j] = _half_rms(k_ref[rows(j), :].astype(F32), kw_ref[...]).astype(BF16)
        vt = v_ref[rows(j), :].astype(F32).T
        ones_row = (lax.broadcasted_iota(jnp.int32, (PV_EXTRA_ROWS, tile), 0) == 0).astype(F32)
        vt_ref[j] = jnp.concatenate([vt, ones_row], axis=0).astype(BF16)
        q = (_half_rms(q_ref[rows(j), :].astype(F32), qw_ref[...]) * q_scale).astype(BF16)
        lane = lax.broadcasted_iota(jnp.int32, q.shape, 1)
        zero = jnp.zeros_like(q)
        qm_ref[j, 0] = jnp.where(lane < ATTN_HEAD_DIM, q, zero)
        qm_ref[j, 1] = jnp.where(lane < ATTN_HEAD_DIM, zero, q)
        m_ref[j] = jnp.full(m_ref.shape[1:], NEG_BIG, F32)
        acc_ref[j] = jnp.zeros(acc_ref.shape[1:], F32)

    def scores(qi, kj, slot, diag):
        k = kn_ref[kj]
        for mp in range(2):
            s = lax.dot_general(k, qm_ref[qi, mp], NT_DIMS, preferred_element_type=F32)
            if diag:
                key = lax.broadcasted_iota(jnp.int32, s.shape, 0)
                qry = lax.broadcasted_iota(jnp.int32, s.shape, 1)
                s = jnp.where(key <= qry, s, NEG_BIG)
            s_ref[slot, mp] = s.astype(BF16)
            mt_ref[slot, mp] = jnp.max(s, axis=0, keepdims=True)

    def softmax_pv(qi, kj, slot):
        vt = vt_ref[kj]
        for mp in range(2):
            m_old = m_ref[qi, mp]
            mt = mt_ref[slot, mp]
            mt_up = (mt + jnp.abs(mt) * BF16_ULP_UP).astype(BF16).astype(F32)
            m_new = jnp.maximum(m_old, mt_up)
            p = jnp.exp2(s_ref[slot, mp] - m_new.astype(BF16))
            alpha = jnp.exp2(m_old - m_new)
            acc_ref[qi, mp] = alpha * acc_ref[qi, mp] + jnp.dot(vt, p, preferred_element_type=F32)
            m_ref[qi, mp] = m_new

    for j in range(n_tiles + 1):
        if j < n_tiles:
            prep(j)
            scores(j, j, j % 2, True)
        if j >= 1:
            softmax_pv(j - 1, j - 1, (j - 1) % 2)

    if n_full:
        unroll = PHASE_B_UNROLL
        n_loop = (n_full - 1) // unroll
        scores(full_sched[0][0], full_sched[0][1], 0, False)

        def body(it, carry):
            base = it * unroll
            steps = [(sched_ref[0, base + u], sched_ref[1, base + u]) for u in range(unroll + 1)]
            for u in range(unroll):
                scores(*steps[u + 1], (u + 1) % 2, False)
                softmax_pv(*steps[u], u % 2)
            return carry

        lax.fori_loop(0, n_loop, body, 0)
        for t in range(n_loop * unroll, n_full):
            if t + 1 < n_full:
                scores(*full_sched[t + 1], (t + 1) % 2, False)
            softmax_pv(*full_sched[t], t % 2)

    lf = lam_ref[...]
    li = li_ref[:, 0:1]
    s01 = jnp.sum(lf[0:1] * lf[1:2], axis=-1, keepdims=True)
    s23 = jnp.sum(lf[2:3] * lf[3:4], axis=-1, keepdims=True)
    lam = jnp.exp(s01) - jnp.exp(s23) + li
    for j in range(n_tiles):
        a0, a1 = acc_ref[j, 0], acc_ref[j, 1]
        hd = ATTN_V_DIM
        o = a0[:hd] / a0[hd:hd + 1] - lam * (a1[:hd] / a1[hd:hd + 1])
        ms = jnp.mean(o * o, axis=0, keepdims=True)
        on = (o * lax.rsqrt(ms + EPS)).T
        z = z_ref[rows(j), :].astype(F32)
        y = on * sw_ref[...] * (1.0 - li) * _silu(z)
        o_ref[rows(j), :] = y.astype(o_ref.dtype)


def _diff_attn(diff_lambda, lam_init, proj, qw, kw, subln_w, *, batch, seq, heads, tile, q_col0, k_col0, v_col0,
               z_col0):
    m = proj.shape[0]
    n_tiles = seq // tile
    q_scale = ATTN_HEAD_DIM ** -0.5 * LOG2E
    sched = jnp.asarray(_full_tile_schedule(n_tiles) or [(0, 0)], jnp.int32).T
    const = lambda b, h, s: (0, 0)
    col = lambda c0: (lambda b, h, s: (b, c0 + h))
    blk = (seq, ATTN_V_DIM)
    grid_spec = pltpu.PrefetchScalarGridSpec(
        num_scalar_prefetch=1,
        grid=(batch, heads),
        in_specs=[
            pl.BlockSpec(diff_lambda.shape, const),
            pl.BlockSpec(lam_init.shape, const),
            pl.BlockSpec(blk, col(q_col0)),
            pl.BlockSpec(blk, col(k_col0)),
            pl.BlockSpec(blk, col(v_col0)),
            pl.BlockSpec(blk, col(z_col0)),
            pl.BlockSpec(qw.shape, const),
            pl.BlockSpec(kw.shape, const),
            pl.BlockSpec(subln_w.shape, const),
        ],
        out_specs=pl.BlockSpec(blk, lambda b, h, s: (b, h)),
        scratch_shapes=[
            pltpu.VMEM((n_tiles, tile, ATTN_V_DIM), BF16),
            pltpu.VMEM((n_tiles, ATTN_V_DIM + PV_EXTRA_ROWS, tile), BF16),
            pltpu.VMEM((n_tiles, 2, tile, ATTN_V_DIM), BF16),
            pltpu.VMEM((2, 2, tile, tile), BF16),
            pltpu.VMEM((2, 2, 1, tile), F32),
            pltpu.VMEM((n_tiles, 2, 1, tile), F32),
            pltpu.VMEM((n_tiles, 2, ATTN_V_DIM + PV_EXTRA_ROWS, tile), F32),
        ],
    )
    return pl.pallas_call(
        functools.partial(_attn_kernel, tile=tile, q_scale=q_scale),
        grid_spec=grid_spec,
        out_shape=jax.ShapeDtypeStruct((m, heads * ATTN_V_DIM), BF16),
        compiler_params=_compiler_params(("parallel", "parallel")),
        name="diff_attn",
    )(sched, diff_lambda, lam_init, proj, proj, proj, proj, qw, kw, subln_w)


def _causal_conv_silu(cur, prev, shift, w_ref, b_ref):
    rows = cur.shape[0]
    ext = jnp.concatenate([prev, cur], axis=0)
    shifted = jnp.dot(shift, ext, preferred_element_type=F32)
    acc = b_ref[...] + w_ref[0:1, :] * shifted[0:rows, :]
    for k in range(1, SSD_CONV):
        acc = acc + w_ref[k:k + 1, :] * shifted[k * rows:(k + 1) * rows, :]
    return _silu(acc)


def _split3(x):
    hi = x.astype(BF16)
    r1 = x - hi.astype(F32)
    mid = r1.astype(BF16)
    lo = (r1 - mid.astype(F32)).astype(BF16)
    return hi, mid, lo


def _ssd_chunk(xbc, zs, dt, a_neg, dskip_ref, nw_ref, state_ref, y_ref, out_rows):
    L = xbc.shape[0]
    N = SSD_STATE
    width = y_ref.shape[1]
    gn = SSD_GROUPS * N
    pair_w = 2 * SSD_HEAD_DIM
    pairs_per_group = width // pair_w // SSD_GROUPS
    xs = xbc[:, :width]
    bm = xbc[:, width:width + gn]
    cm = xbc[:, width + gn:]

    dta = dt * a_neg
    row = lax.broadcasted_iota(jnp.int32, (L, L), 0)
    col = lax.broadcasted_iota(jnp.int32, (L, L), 1)
    causal = row >= col
    tri = jnp.where(causal, 1.0, 0.0).astype(BF16)
    a_cum = sum(jnp.dot(tri, part, preferred_element_type=F32) for part in _split3(dta))
    a_cum_t = a_cum.T
    dt_t = dt.T
    a_src_t = a_cum_t - jnp.log2(dt_t)

    xs_bf = xs.astype(BF16)
    lane = lax.broadcasted_iota(jnp.int32, (L, pair_w), 1)
    left = lane < SSD_HEAD_DIM

    y_slabs = []
    for g in range(SSD_GROUPS):
        bg = bm[:, g * N:(g + 1) * N]
        cg = cm[:, g * N:(g + 1) * N]
        cb = lax.dot_general(cg.astype(BF16), bg.astype(BF16), NT_DIMS, preferred_element_type=F32)
        cb = jnp.where(causal, cb, 0.0)
        bg_t = bg.T
        for jp in range(pairs_per_group):
            pr = g * pairs_per_group + jp
            lanes = slice(pr * pair_w, (pr + 1) * pair_w)
            x_pair = xs_bf[:, lanes]
            s_pair = state_ref[:, lanes]
            lhs_y, lhs_s, chunk_decay = [], [], []
            for e in range(2):
                h = 2 * pr + e
                colb = jnp.broadcast_to(a_cum[:, h:h + 1], (L, L))
                rowb = a_cum_t[h:h + 1, :]
                m_h = cb * jnp.exp2(jnp.minimum(colb - a_src_t[h:h + 1, :], DECAY_EXP_CLAMP))
                cw_h = cg * jnp.exp2(colb)
                lhs_y.append(jnp.concatenate([m_h, cw_h], axis=1).astype(BF16))
                a_last = colb[L - 1:L, :]
                wrow = dt_t[h:h + 1, :] * jnp.exp2(a_last - rowb)
                lhs_s.append((bg_t * wrow).astype(BF16))
                chunk_decay.append(jnp.exp2(a_last))
            rhs_y = jnp.concatenate([x_pair, s_pair.astype(BF16)], axis=0)
            y2 = jnp.dot(jnp.concatenate(lhs_y, axis=0), rhs_y, preferred_element_type=F32)
            y_slabs.append(jnp.where(left, y2[:L], y2[L:]))
            s2 = jnp.dot(jnp.concatenate(lhs_s, axis=0), x_pair, preferred_element_type=F32)
            state_ref[:, lanes] = (jnp.where(left, s2[:N], s2[N:])
                                   + s_pair * jnp.where(left, chunk_decay[0], chunk_decay[1]))

    y = jnp.concatenate(y_slabs, axis=1) + xs * dskip_ref[...]
    gated = y * _silu(zs.astype(F32))
    gw = width // SSD_GROUPS
    for g in range(SSD_GROUPS):
        blk = gated[:, g * gw:(g + 1) * gw]
        ms = jnp.mean(blk * blk, axis=-1, keepdims=True)
        y_ref[out_rows, g * gw:(g + 1) * gw] = (blk * lax.rsqrt(ms + EPS)
                                                * nw_ref[:, g * gw:(g + 1) * gw]).astype(y_ref.dtype)


def _ssd_kernel(xbc_ref, zs_ref, dt_ref, cw_ref, cb_ref, alog_ref, dskip_ref, nw_ref, shift_ref,
                y_ref,
                prev_ref, state_ref):
    c = pl.program_id(1)
    L = SSD_CHUNK

    @pl.when(c == 0)
    def _():
        prev_ref[...] = jnp.zeros(prev_ref.shape, prev_ref.dtype)
        state_ref[...] = jnp.zeros(state_ref.shape, F32)

    a_neg = -jnp.exp(alog_ref[...]) * LOG2E
    shift = shift_ref[...]
    prev = prev_ref[...]
    for sub in range(xbc_ref.shape[0] // L):
        rows = slice(sub * L, (sub + 1) * L)
        cur = xbc_ref[rows, :]
        xbc = _causal_conv_silu(cur, prev, shift, cw_ref, cb_ref)
        _ssd_chunk(xbc, zs_ref[rows, :], dt_ref[rows, :], a_neg, dskip_ref, nw_ref, state_ref, y_ref, rows)
        prev = cur
    prev_ref[...] = prev


def _ssd(proj, dt, conv_w, conv_b, a_log, d_skip, norm_w, *, batch, seq, rows_per_step, xbc_col, zs_col):
    assert SSD_CHUNK == SSD_STATE == LANES, "the chunk kernel reuses (L, L) tiles as (L, N) and head rows as lanes"
    m = proj.shape[0]
    L = SSD_CHUNK
    R = rows_per_step
    assert R % L == 0 and seq % R == 0
    nc = seq // R
    conv_dim = conv_w.shape[1]
    width = norm_w.shape[1]
    row = lambda b, c: b * nc + c
    const = lambda b, c: (0, 0)
    tap_row = jnp.arange(SSD_CONV * L)[:, None]
    shift = (jnp.arange(2 * L)[None, :] == tap_row % L + L - (SSD_CONV - 1) + tap_row // L).astype(BF16)
    return pl.pallas_call(
        _ssd_kernel,
        grid=(batch, nc),
        in_specs=[
            pl.BlockSpec((R, conv_dim), lambda b, c: (row(b, c), xbc_col)),
            pl.BlockSpec((R, width), lambda b, c: (row(b, c), zs_col)),
            pl.BlockSpec((R, LANES), lambda b, c: (row(b, c), 0)),
            pl.BlockSpec(conv_w.shape, const), pl.BlockSpec(conv_b.shape, const),
            pl.BlockSpec(a_log.shape, const), pl.BlockSpec(d_skip.shape, const),
            pl.BlockSpec(norm_w.shape, const),
            pl.BlockSpec(shift.shape, const),
        ],
        out_specs=pl.BlockSpec((R, width), lambda b, c: (row(b, c), 0)),
        out_shape=jax.ShapeDtypeStruct((m, width), BF16),
        scratch_shapes=[
            pltpu.VMEM((L, conv_dim), BF16),
            pltpu.VMEM((SSD_STATE, width), F32),
        ],
        compiler_params=_compiler_params(("parallel", "arbitrary")),
        name="ssd",
    )(proj, proj, dt, conv_w, conv_b, a_log, d_skip, norm_w, shift)


def _outproj_kernel(ya_ref, ys_ref, ga_ref, gs_ref, x_ref, wpa_ref, wps_ref, wo_ref, o_ref):
    pa = jnp.dot(ya_ref[...], wpa_ref[...], preferred_element_type=F32)
    ps = jnp.dot(ys_ref[...], wps_ref[...], preferred_element_type=F32)
    merged = (jax.nn.sigmoid(ga_ref[...].astype(F32)) * pa
              + jax.nn.sigmoid(gs_ref[...].astype(F32)) * ps)
    o_ref[...] = x_ref[...] + jnp.dot(merged.astype(BF16), wo_ref[...], preferred_element_type=F32)


def _out_proj(y_a, y_s, proj, x, w_pa, w_ps, w_o, *, tm, ga_col, gs_col):
    m, d = x.shape
    const = lambda i: (0, 0)
    return pl.pallas_call(
        _outproj_kernel,
        grid=(m // tm,),
        in_specs=[
            pl.BlockSpec((tm, y_a.shape[1]), lambda i: (i, 0)),
            pl.BlockSpec((tm, y_s.shape[1]), lambda i: (i, 0)),
            pl.BlockSpec((tm, d), lambda i: (i, ga_col)),
            pl.BlockSpec((tm, d), lambda i: (i, gs_col)),
            pl.BlockSpec((tm, d), lambda i: (i, 0)),
            pl.BlockSpec(w_pa.shape, const),
            pl.BlockSpec(w_ps.shape, const),
            pl.BlockSpec(w_o.shape, const),
        ],
        out_specs=pl.BlockSpec((tm, d), lambda i: (i, 0)),
        out_shape=jax.ShapeDtypeStruct((m, d), F32),
        compiler_params=_compiler_params(("parallel",)),
        name="out_proj",
    )(y_a, y_s, proj, proj, x, w_pa, w_ps, w_o)


def _lambda_init(layer_idx):
    return 0.8 - 0.6 * math.exp(-0.3 * layer_idx)


def kernel(x, norm_w, w_in, q_norm_w, k_norm_w, diff_lambda, subln_w, conv_w, conv_b, dt_bias, a_log, d_skip,
           ssd_norm_w, w_proj_attn, w_proj_ssd, w_out):
    batch, seq, d_model = x.shape
    depth = w_in.shape[0]
    attn_w = w_proj_attn.shape[1]
    ssd_w = w_proj_ssd.shape[1]
    ssd_heads = dt_bias.shape[1]
    heads = attn_w // ATTN_V_DIM
    gn = SSD_GROUPS * SSD_STATE

    o_za = 3 * attn_w
    o_xs = 4 * attn_w
    o_b = o_xs + ssd_w
    o_c = o_b + gn
    o_zs = o_c + gn
    o_dt = o_zs + ssd_w
    o_gate = o_dt + ssd_heads
    w_main = jnp.concatenate([w_in[:, :, :o_xs], w_in[:, :, o_zs:o_dt], w_in[:, :, o_xs:o_zs],
                              w_in[:, :, o_gate:]], axis=-1).astype(BF16)
    w_dt = jnp.pad(w_in[:, :, o_dt:o_gate], ((0, 0), (0, 0), (0, LANES - ssd_heads))).astype(BF16)
    pad_h = ((0, 0), (0, LANES - ssd_heads))
    dt_bias_p = jnp.pad(dt_bias, pad_h)[:, None, :]
    a_log_p = jnp.pad(a_log, pad_h)[:, None, :]
    d_skip_x = jnp.repeat(d_skip, SSD_HEAD_DIM, axis=-1)[:, None, :]
    qw = jnp.tile(q_norm_w, (1, 2))[:, None, :]
    kw = jnp.tile(k_norm_w, (1, 2))[:, None, :]
    w_pa = w_proj_attn.astype(BF16)
    w_ps = w_proj_ssd.astype(BF16)
    w_o = w_out.astype(BF16)

    conv_dim = ssd_w + 2 * gn
    c_zs = o_xs // ssd_w
    c_xbc = (o_xs + ssd_w) // conv_dim
    c_ga = (o_xs + ssd_w + conv_dim) // d_model
    c_gs = c_ga + 1
    c_za = o_za // ATTN_V_DIM
    assert (o_xs + ssd_w) % conv_dim == 0 and o_xs % ssd_w == 0 and (o_xs + ssd_w + conv_dim) % d_model == 0

    rows = batch * seq
    xf = x.reshape(rows, d_model)
    for l in range(depth):
        proj, dt = _in_proj(xf, norm_w[l][None, :], w_main[l], w_dt[l], dt_bias_p[l],
                            tm=min(IN_PROJ_TM, rows), tn=IN_PROJ_TN)
        lam_init = jnp.full((1, LANES), _lambda_init(l), F32)
        y_a = _diff_attn(diff_lambda[l], lam_init, proj, qw[l], kw[l], subln_w[l][None, :], batch=batch, seq=seq,
                         heads=heads, tile=min(ATTN_TILE, seq), q_col0=0, k_col0=heads, v_col0=2 * heads,
                         z_col0=c_za)
        y_s = _ssd(proj, dt, conv_w[l], conv_b[l][None, :], a_log_p[l], d_skip_x[l], ssd_norm_w[l][None, :],
                   batch=batch, seq=seq, rows_per_step=min(SSD_ROWS_PER_STEP, seq), xbc_col=c_xbc, zs_col=c_zs)
        xf = _out_proj(y_a, y_s, proj, xf, w_pa[l], w_ps[l], w_o[l], tm=min(OUT_PROJ_TM, rows),
                       ga_col=c_ga, gs_col=c_gs)
    return xf.reshape(batch, seq, d_model)
```

```python
import functools
import math

import jax
import jax.numpy as jnp
from jax import lax
from jax.experimental import pallas as pl
from jax.experimental.pallas import tpu as pltpu

F32 = jnp.float32
BF16 = jnp.bfloat16

EPS = 1e-6
LANES = 128
ATTN_HEAD_DIM = 64
ATTN_V_DIM = 2 * ATTN_HEAD_DIM
SSD_HEAD_DIM = 64
SSD_GROUPS = 4
SSD_STATE = 128
SSD_CHUNK = 128
SSD_CONV = 4
NEG_BIG = -1e30
DECAY_EXP_CLAMP = 64.0
LOG2E = math.log2(math.e)

NT_DIMS = (((1,), (1,)), ((), ()))
PV_EXTRA_ROWS = 16
PHASE_B_UNROLL = 8
BF16_ULP_UP = 2.0 ** -7

IN_PROJ_TM = 1024
IN_PROJ_TN = 2816
ATTN_TILE = 512
OUT_PROJ_TM = 512
SSD_ROWS_PER_STEP = 256
VMEM_LIMIT_MIB = 48


def _compiler_params(semantics):
    return pltpu.CompilerParams(dimension_semantics=semantics,
                                vmem_limit_bytes=VMEM_LIMIT_MIB * 1024 * 1024)


def _silu(x):
    h = 0.5 * x
    return h + h * jnp.tanh(h)


def _inproj_kernel(x_ref, nw_ref, w_ref, wdt_ref, dtb_ref, proj_ref, dt_ref, h_ref, *, row_chunk):
    j = pl.program_id(1)

    @pl.when(j == 0)
    def _():
        tm = x_ref.shape[0]
        for r in range(tm // row_chunk):
            rows = pl.ds(r * row_chunk, row_chunk)
            x = x_ref[rows, :]
            ms = jnp.mean(x * x, axis=-1, keepdims=True)
            h = (x * lax.rsqrt(ms + EPS) * nw_ref[...]).astype(BF16)
            h_ref[rows, :] = h
            raw = jnp.dot(h, wdt_ref[...], preferred_element_type=F32) + dtb_ref[...]
            dt_ref[rows, :] = jnp.maximum(raw, 0.0) + jnp.log1p(jnp.exp(-jnp.abs(raw)))

    proj_ref[...] = jnp.dot(h_ref[...], w_ref[...], preferred_element_type=F32).astype(proj_ref.dtype)


def _in_proj(x, norm_w, w_main, w_dt, dt_bias, *, tm, tn):
    m, d = x.shape
    n = w_main.shape[1]
    grid = (m // tm, n // tn)
    return pl.pallas_call(
        functools.partial(_inproj_kernel, row_chunk=min(256, tm)),
        grid=grid,
        in_specs=[
            pl.BlockSpec((tm, d), lambda i, j: (i, 0)),
            pl.BlockSpec((1, d), lambda i, j: (0, 0)),
            pl.BlockSpec((d, tn), lambda i, j: (0, j)),
            pl.BlockSpec((d, LANES), lambda i, j: (0, 0)),
            pl.BlockSpec((1, LANES), lambda i, j: (0, 0)),
        ],
        out_specs=[
            pl.BlockSpec((tm, tn), lambda i, j: (i, j)),
            pl.BlockSpec((tm, LANES), lambda i, j: (i, 0)),
        ],
        out_shape=[
            jax.ShapeDtypeStruct((m, n), BF16),
            jax.ShapeDtypeStruct((m, LANES), F32),
        ],
        scratch_shapes=[pltpu.VMEM((tm, d), BF16)],
        compiler_params=_compiler_params(("parallel", "arbitrary")),
        name="in_proj",
    )(x, norm_w, w_main, w_dt, dt_bias)


def _half_rms(x, w):
    lane = lax.broadcasted_iota(jnp.int32, x.shape, 1)
    left = lane < ATTN_HEAD_DIM
    x2 = x * x
    s_left = jnp.sum(jnp.where(left, x2, 0.0), axis=-1, keepdims=True)
    s_right = jnp.sum(jnp.where(left, 0.0, x2), axis=-1, keepdims=True)
    ms = jnp.where(left, s_left, s_right) * (1.0 / ATTN_HEAD_DIM)
    return x * lax.rsqrt(ms + EPS) * w


def _full_tile_schedule(n_tiles):
    return [(qi, kj) for qi in range(n_tiles) for kj in range(qi)]


def _attn_kernel(sched_ref, lam_ref, li_ref, q_ref, k_ref, v_ref, z_ref, qw_ref, kw_ref, sw_ref, o_ref,
                 kn_ref, vt_ref, qm_ref, s_ref, mt_ref, m_ref, acc_ref, *, tile, q_scale):
    n_tiles = q_ref.shape[0] // tile
    full_sched = _full_tile_schedule(n_tiles)
    n_full = len(full_sched)
    rows = lambda i: slice(i * tile, (i + 1) * tile)

    def prep(j):
        kn_ref[j] = _half_rms(k_ref[rows(j), :].astype(F32), kw_ref[...]).astype(BF16)
        vt = v_ref[rows(j), :].astype(F32).T
        ones_row = (lax.broadcasted_iota(jnp.int32, (PV_EXTRA_ROWS, tile), 0) == 0).astype(F32)
        vt_ref[j] = jnp.concatenate([vt, ones_row], axis=0).astype(BF16)
        q = (_half_rms(q_ref[rows(j), :].astype(F32), qw_ref[...]) * q_scale).astype(BF16)
        lane = lax.broadcasted_iota(jnp.int32, q.shape, 1)
        zero = jnp.zeros_like(q)
        qm_ref[j, 0] = jnp.where(lane < ATTN_HEAD_DIM, q, zero)
        qm_ref[j, 1] = jnp.where(lane < ATTN_HEAD_DIM, zero, q)
        m_ref[j] = jnp.full(m_ref.shape[1:], NEG_BIG, F32)
        acc_ref[j] = jnp.zeros(acc_ref.shape[1:], F32)

    def scores(qi, kj, slot, diag):
        k = kn_ref[kj]
        for mp in range(2):
            s = lax.dot_general(k, qm_ref[qi, mp], NT_DIMS, preferred_element_type=F32)
            if diag:
                key = lax.broadcasted_iota(jnp.int32, s.shape, 0)
                qry = lax.broadcasted_iota(jnp.int32, s.shape, 1)
                s = jnp.where(key <= qry, s, NEG_BIG)
            s_ref[slot, mp] = s.astype(BF16)
            mt_ref[slot, mp] = jnp.max(s, axis=0, keepdims=True)

    def softmax_pv(qi, kj, slot):
        vt = vt_ref[kj]
        for mp in range(2):
            m_old = m_ref[qi, mp]
            mt = mt_ref[slot, mp]
            mt_up = (mt + jnp.abs(mt) * BF16_ULP_UP).astype(BF16).astype(F32)
            m_new = jnp.maximum(m_old, mt_up)
            p = jnp.exp2(s_ref[slot, mp] - m_new.astype(BF16))
            alpha = jnp.exp2(m_old - m_new)
            acc_ref[qi, mp] = alpha * acc_ref[qi, mp] + jnp.dot(vt, p, preferred_element_type=F32)
            m_ref[qi, mp] = m_new

    for j in range(n_tiles + 1):
        if j < n_tiles:
            prep(j)
            scores(j, j, j % 2, True)
        if j >= 1:
            softmax_pv(j - 1, j - 1, (j - 1) % 2)

    if n_full:
        unroll = PHASE_B_UNROLL
        n_loop = (n_full - 1) // unroll
        scores(full_sched[0][0], full_sched[0][1], 0, False)

        def body(it, carry):
            base = it * unroll
            steps = [(sched_ref[0, base + u], sched_ref[1, base + u]) for u in range(unroll + 1)]
            for u in range(unroll):
                scores(*steps[u + 1], (u + 1) % 2, False)
                softmax_pv(*steps[u], u % 2)
            return carry

        lax.fori_loop(0, n_loop, body, 0)
        for t in range(n_loop * unroll, n_full):
            if t + 1 < n_full:
                scores(*full_sched[t + 1], (t + 1) % 2, False)
            softmax_pv(*full_sched[t], t % 2)

    lf = lam_ref[...]
    li = li_ref[:, 0:1]
    s01 = jnp.sum(lf[0:1] * lf[1:2], axis=-1, keepdims=True)
    s23 = jnp.sum(lf[2:3] * lf[3:4], axis=-1, keepdims=True)
    lam = jnp.exp(s01) - jnp.exp(s23) + li
    for j in range(n_tiles):
        a0, a1 = acc_ref[j, 0], acc_ref[j, 1]
        hd = ATTN_V_DIM
        o = a0[:hd] / a0[hd:hd + 1] - lam * (a1[:hd] / a1[hd:hd + 1])
        ms = jnp.mean(o * o, axis=0, keepdims=True)
        on = (o * lax.rsqrt(ms + EPS)).T
        z = z_ref[rows(j), :].astype(F32)
        y = on * sw_ref[...] * (1.0 - li) * _silu(z)
        o_ref[rows(j), :] = y.astype(o_ref.dtype)


def _diff_attn(diff_lambda, lam_init, proj, qw, kw, subln_w, *, batch, seq, heads, tile, q_col0, k_col0, v_col0,
               z_col0):
    m = proj.shape[0]
    n_tiles = seq // tile
    q_scale = ATTN_HEAD_DIM ** -0.5 * LOG2E
    sched = jnp.asarray(_full_tile_schedule(n_tiles) or [(0, 0)], jnp.int32).T
    const = lambda b, h, s: (0, 0)
    col = lambda c0: (lambda b, h, s: (b, c0 + h))
    blk = (seq, ATTN_V_DIM)
    grid_spec = pltpu.PrefetchScalarGridSpec(
        num_scalar_prefetch=1,
        grid=(batch, heads),
        in_specs=[
            pl.BlockSpec(diff_lambda.shape, const),
            pl.BlockSpec(lam_init.shape, const),
            pl.BlockSpec(blk, col(q_col0)),
            pl.BlockSpec(blk, col(k_col0)),
            pl.BlockSpec(blk, col(v_col0)),
            pl.BlockSpec(blk, col(z_col0)),
            pl.BlockSpec(qw.shape, const),
            pl.BlockSpec(kw.shape, const),
            pl.BlockSpec(subln_w.shape, const),
        ],
        out_specs=pl.BlockSpec(blk, lambda b, h, s: (b, h)),
        scratch_shapes=[
            pltpu.VMEM((n_tiles, tile, ATTN_V_DIM), BF16),
            pltpu.VMEM((n_tiles, ATTN_V_DIM + PV_EXTRA_ROWS, tile), BF16),
            pltpu.VMEM((n_tiles, 2, tile, ATTN_V_DIM), BF16),
            pltpu.VMEM((2, 2, tile, tile), BF16),
            pltpu.VMEM((2, 2, 1, tile), F32),
            pltpu.VMEM((n_tiles, 2, 1, tile), F32),
            pltpu.VMEM((n_tiles, 2, ATTN_V_DIM + PV_EXTRA_ROWS, tile), F32),
        ],
    )
    return pl.pallas_call(
        functools.partial(_attn_kernel, tile=tile, q_scale=q_scale),
        grid_spec=grid_spec,
        out_shape=jax.ShapeDtypeStruct((m, heads * ATTN_V_DIM), BF16),
        compiler_params=_compiler_params(("parallel", "parallel")),
        name="diff_attn",
    )(sched, diff_lambda, lam_init, proj, proj, proj, proj, qw, kw, subln_w)


def _causal_conv_silu(cur, prev, shift, w_ref, b_ref):
    rows = cur.shape[0]
    ext = jnp.concatenate([prev, cur], axis=0)
    shifted = jnp.dot(shift, ext, preferred_element_type=F32)
    acc = b_ref[...] + w_ref[0:1, :] * shifted[0:rows, :]
    for k in range(1, SSD_CONV):
        acc = acc + w_ref[k:k + 1, :] * shifted[k * rows:(k + 1) * rows, :]
    return _silu(acc)


def _split3(x):
    hi = x.astype(BF16)
    r1 = x - hi.astype(F32)
    mid = r1.astype(BF16)
    lo = (r1 - mid.astype(F32)).astype(BF16)
    return hi, mid, lo


def _ssd_chunk(xbc, zs, dt, a_neg, dskip_ref, nw_ref, state_ref, y_ref, out_rows):
    L = xbc.shape[0]
    N = SSD_STATE
    width = y_ref.shape[1]
    gn = SSD_GROUPS * N
    pair_w = 2 * SSD_HEAD_DIM
    pairs_per_group = width // pair_w // SSD_GROUPS
    xs = xbc[:, :width]
    bm = xbc[:, width:width + gn]
    cm = xbc[:, width + gn:]

    dta = dt * a_neg
    row = lax.broadcasted_iota(jnp.int32, (L, L), 0)
    col = lax.broadcasted_iota(jnp.int32, (L, L), 1)
    causal = row >= col
    tri = jnp.where(causal, 1.0, 0.0).astype(BF16)
    a_cum = sum(jnp.dot(tri, part, preferred_element_type=F32) for part in _split3(dta))
    a_cum_t = a_cum.T
    dt_t = dt.T
    a_src_t = a_cum_t - jnp.log2(dt_t)

    xs_bf = xs.astype(BF16)
    lane = lax.broadcasted_iota(jnp.int32, (L, pair_w), 1)
    left = lane < SSD_HEAD_DIM

    y_slabs = []
    for g in range(SSD_GROUPS):
        bg = bm[:, g * N:(g + 1) * N]
        cg = cm[:, g * N:(g + 1) * N]
        cb = lax.dot_general(cg.astype(BF16), bg.astype(BF16), NT_DIMS, preferred_element_type=F32)
        cb = jnp.where(causal, cb, 0.0)
        bg_t = bg.T
        for jp in range(pairs_per_group):
            pr = g * pairs_per_group + jp
            lanes = slice(pr * pair_w, (pr + 1) * pair_w)
            x_pair = xs_bf[:, lanes]
            s_pair = state_ref[:, lanes]
            lhs_y, lhs_s, chunk_decay = [], [], []
            for e in range(2):
                h = 2 * pr + e
                colb = jnp.broadcast_to(a_cum[:, h:h + 1], (L, L))
                rowb = a_cum_t[h:h + 1, :]
                m_h = cb * jnp.exp2(jnp.minimum(colb - a_src_t[h:h + 1, :], DECAY_EXP_CLAMP))
                cw_h = cg * jnp.exp2(colb)
                lhs_y.append(jnp.concatenate([m_h, cw_h], axis=1).astype(BF16))
                a_last = colb[L - 1:L, :]
                wrow = dt_t[h:h + 1, :] * jnp.exp2(a_last - rowb)
                lhs_s.append((bg_t * wrow).astype(BF16))
                chunk_decay.append(jnp.exp2(a_last))
            rhs_y = jnp.concatenate([x_pair, s_pair.astype(BF16)], axis=0)
            y2 = jnp.dot(jnp.concatenate(lhs_y, axis=0), rhs_y, preferred_element_type=F32)
            y_slabs.append(jnp.where(left, y2[:L], y2[L:]))
            s2 = jnp.dot(jnp.concatenate(lhs_s, axis=0), x_pair, preferred_element_type=F32)
            state_ref[:, lanes] = (jnp.where(left, s2[:N], s2[N:])
                                   + s_pair * jnp.where(left, chunk_decay[0], chunk_decay[1]))

    y = jnp.concatenate(y_slabs, axis=1) + xs * dskip_ref[...]
    gated = y * _silu(zs.astype(F32))
    gw = width // SSD_GROUPS
    for g in range(SSD_GROUPS):
        blk = gated[:, g * gw:(g + 1) * gw]
        ms = jnp.mean(blk * blk, axis=-1, keepdims=True)
        y_ref[out_rows, g * gw:(g + 1) * gw] = (blk * lax.rsqrt(ms + EPS)
                                                * nw_ref[:, g * gw:(g + 1) * gw]).astype(y_ref.dtype)


def _ssd_kernel(xbc_ref, zs_ref, dt_ref, cw_ref, cb_ref, alog_ref, dskip_ref, nw_ref, shift_ref,
                y_ref,
                prev_ref, state_ref):
    c = pl.program_id(1)
    L = SSD_CHUNK

    @pl.when(c == 0)
    def _():
        prev_ref[...] = jnp.zeros(prev_ref.shape, prev_ref.dtype)
        state_ref[...] = jnp.zeros(state_ref.shape, F32)

    a_neg = -jnp.exp(alog_ref[...]) * LOG2E
    shift = shift_ref[...]
    prev = prev_ref[...]
    for sub in range(xbc_ref.shape[0] // L):
        rows = slice(sub * L, (sub + 1) * L)
        cur = xbc_ref[rows, :]
        xbc = _causal_conv_silu(cur, prev, shift, cw_ref, cb_ref)
        _ssd_chunk(xbc, zs_ref[rows, :], dt_ref[rows, :], a_neg, dskip_ref, nw_ref, state_ref, y_ref, rows)
        prev = cur
    prev_ref[...] = prev


def _ssd(proj, dt, conv_w, conv_b, a_log, d_skip, norm_w, *, batch, seq, rows_per_step, xbc_col, zs_col):
    assert SSD_CHUNK == SSD_STATE == LANES, "the chunk kernel reuses (L, L) tiles as (L, N) and head rows as lanes"
    m = proj.shape[0]
    L = SSD_CHUNK
    R = rows_per_step
    assert R % L == 0 and seq % R == 0
    nc = seq // R
    conv_dim = conv_w.shape[1]
    width = norm_w.shape[1]
    row = lambda b, c: b * nc + c
    const = lambda b, c: (0, 0)
    tap_row = jnp.arange(SSD_CONV * L)[:, None]
    shift = (jnp.arange(2 * L)[None, :] == tap_row % L + L - (SSD_CONV - 1) + tap_row // L).astype(BF16)
    return pl.pallas_call(
        _ssd_kernel,
        grid=(batch, nc),
        in_specs=[
            pl.BlockSpec((R, conv_dim), lambda b, c: (row(b, c), xbc_col)),
            pl.BlockSpec((R, width), lambda b, c: (row(b, c), zs_col)),
            pl.BlockSpec((R, LANES), lambda b, c: (row(b, c), 0)),
            pl.BlockSpec(conv_w.shape, const), pl.BlockSpec(conv_b.shape, const),
            pl.BlockSpec(a_log.shape, const), pl.BlockSpec(d_skip.shape, const),
            pl.BlockSpec(norm_w.shape, const),
            pl.BlockSpec(shift.shape, const),
        ],
        out_specs=pl.BlockSpec((R, width), lambda b, c: (row(b, c), 0)),
        out_shape=jax.ShapeDtypeStruct((m, width), BF16),
        scratch_shapes=[
            pltpu.VMEM((L, conv_dim), BF16),
            pltpu.VMEM((SSD_STATE, width), F32),
        ],
        compiler_params=_compiler_params(("parallel", "arbitrary")),
        name="ssd",
    )(proj, proj, dt, conv_w, conv_b, a_log, d_skip, norm_w, shift)


def _outproj_kernel(ya_ref, ys_ref, ga_ref, gs_ref, x_ref, wpa_ref, wps_ref, wo_ref, o_ref):
    pa = jnp.dot(ya_ref[...], wpa_ref[...], preferred_element_type=F32)
    ps = jnp.dot(ys_ref[...], wps_ref[...], preferred_element_type=F32)
    merged = (jax.nn.sigmoid(ga_ref[...].astype(F32)) * pa
              + jax.nn.sigmoid(gs_ref[...].astype(F32)) * ps)
    o_ref[...] = x_ref[...] + jnp.dot(merged.astype(BF16), wo_ref[...], preferred_element_type=F32)


def _out_proj(y_a, y_s, proj, x, w_pa, w_ps, w_o, *, tm, ga_col, gs_col):
    m, d = x.shape
    const = lambda i: (0, 0)
    return pl.pallas_call(
        _outproj_kernel,
        grid=(m // tm,),
        in_specs=[
            pl.BlockSpec((tm, y_a.shape[1]), lambda i: (i, 0)),
            pl.BlockSpec((tm, y_s.shape[1]), lambda i: (i, 0)),
            pl.BlockSpec((tm, d), lambda i: (i, ga_col)),
            pl.BlockSpec((tm, d), lambda i: (i, gs_col)),
            pl.BlockSpec((tm, d), lambda i: (i, 0)),
            pl.BlockSpec(w_pa.shape, const),
            pl.BlockSpec(w_ps.shape, const),
            pl.BlockSpec(w_o.shape, const),
        ],
        out_specs=pl.BlockSpec((tm, d), lambda i: (i, 0)),
        out_shape=jax.ShapeDtypeStruct((m, d), F32),
        compiler_params=_compiler_params(("parallel",)),
        name="out_proj",
    )(y_a, y_s, proj, proj, x, w_pa, w_ps, w_o)


def _lambda_init(layer_idx):
    return 0.8 - 0.6 * math.exp(-0.3 * layer_idx)


def kernel(x, norm_w, w_in, q_norm_w, k_norm_w, diff_lambda, subln_w, conv_w, conv_b, dt_bias, a_log, d_skip,
           ssd_norm_w, w_proj_attn, w_proj_ssd, w_out):
    batch, seq, d_model = x.shape
    depth = w_in.shape[0]
    attn_w = w_proj_attn.shape[1]
    ssd_w = w_proj_ssd.shape[1]
    ssd_heads = dt_bias.shape[1]
    heads = attn_w // ATTN_V_DIM
    gn = SSD_GROUPS * SSD_STATE

    o_za = 3 * attn_w
    o_xs = 4 * attn_w
    o_b = o_xs + ssd_w
    o_c = o_b + gn
    o_zs = o_c + gn
    o_dt = o_zs + ssd_w
    o_gate = o_dt + ssd_heads
    w_bf = w_in.astype(BF16)
    w_main = jnp.concatenate([w_bf[:, :, :o_xs], w_bf[:, :, o_zs:o_dt], w_bf[:, :, o_xs:o_zs],
                              w_bf[:, :, o_gate:]], axis=-1)
    w_dt = jnp.pad(w_bf[:, :, o_dt:o_gate], ((0, 0), (0, 0), (0, LANES - ssd_heads)))
    pad_h = ((0, 0), (0, LANES - ssd_heads))
    dt_bias_p = jnp.pad(dt_bias, pad_h)[:, None, :]
    a_log_p = jnp.pad(a_log, pad_h)[:, None, :]
    d_skip_x = jnp.repeat(d_skip, SSD_HEAD_DIM, axis=-1)[:, None, :]
    qw = jnp.tile(q_norm_w, (1, 2))[:, None, :]
    kw = jnp.tile(k_norm_w, (1, 2))[:, None, :]
    w_pa = w_proj_attn.astype(BF16)
    w_ps = w_proj_ssd.astype(BF16)
    w_o = w_out.astype(BF16)

    conv_dim = ssd_w + 2 * gn
    c_zs = o_xs // ssd_w
    c_xbc = (o_xs + ssd_w) // conv_dim
    c_ga = (o_xs + ssd_w + conv_dim) // d_model
    c_gs = c_ga + 1
    c_za = o_za // ATTN_V_DIM
    assert (o_xs + ssd_w) % conv_dim == 0 and o_xs % ssd_w == 0 and (o_xs + ssd_w + conv_dim) % d_model == 0

    rows = batch * seq
    xf = x.reshape(rows, d_model)
    for l in range(depth):
        proj, dt = _in_proj(xf, norm_w[l][None, :], w_main[l], w_dt[l], dt_bias_p[l],
                            tm=min(IN_PROJ_TM, rows), tn=IN_PROJ_TN)
        lam_init = jnp.full((1, LANES), _lambda_init(l), F32)
        y_a = _diff_attn(diff_lambda[l], lam_init, proj, qw[l], kw[l], subln_w[l][None, :], batch=batch, seq=seq,
                         heads=heads, tile=min(ATTN_TILE, seq), q_col0=0, k_col0=heads, v_col0=2 * heads,
                         z_col0=c_za)
        y_s = _ssd(proj, dt, conv_w[l], conv_b[l][None, :], a_log_p[l], d_skip_x[l], ssd_norm_w[l][None, :],
                   batch=batch, seq=seq, rows_per_step=min(SSD_ROWS_PER_STEP, seq), xbc_col=c_xbc, zs_col=c_zs)
        xf = _out_proj(y_a, y_s, proj, xf, w_pa[l], w_ps[l], w_o[l], tm=min(OUT_PROJ_TM, rows),
                       ga_col=c_ga, gs_col=c_gs)
    return xf.reshape(batch, seq, d_model)
```

```python
import functools
import math

import jax
import jax.numpy as jnp
from jax import lax
from jax.experimental import pallas as pl
from jax.experimental.pallas import tpu as pltpu

F32 = jnp.float32
BF16 = jnp.bfloat16

EPS = 1e-6
LANES = 128
ATTN_HEAD_DIM = 64
ATTN_V_DIM = 2 * ATTN_HEAD_DIM
SSD_HEAD_DIM = 64
SSD_GROUPS = 4
SSD_STATE = 128
SSD_CHUNK = 128
SSD_CONV = 4
NEG_BIG = -1e30
DECAY_EXP_CLAMP = 64.0
LOG2E = math.log2(math.e)

NT_DIMS = (((1,), (1,)), ((), ()))
PV_EXTRA_ROWS = 16
PHASE_B_UNROLL = 8
BF16_ULP_UP = 2.0 ** -7

IN_PROJ_TM = 1024
IN_PROJ_TN = 2816
ATTN_TILE = 512
OUT_PROJ_TM = 512
SSD_ROWS_PER_STEP = 256
VMEM_LIMIT_MIB = 48


def _compiler_params(semantics):
    return pltpu.CompilerParams(dimension_semantics=semantics,
                                vmem_limit_bytes=VMEM_LIMIT_MIB * 1024 * 1024)


def _silu(x):
    h = 0.5 * x
    return h + h * jnp.tanh(h)


def _inproj_kernel(x_ref, nw_ref, w_ref, wdt_ref, dtb_ref, proj_ref, dt_ref, h_ref, *, row_chunk):
    j = pl.program_id(1)

    @pl.when(j == 0)
    def _():
        tm = x_ref.shape[0]
        for r in range(tm // row_chunk):
            rows = pl.ds(r * row_chunk, row_chunk)
            x = x_ref[rows, :]
            ms = jnp.mean(x * x, axis=-1, keepdims=True)
            h = (x * lax.rsqrt(ms + EPS) * nw_ref[...]).astype(BF16)
            h_ref[rows, :] = h
            raw = jnp.dot(h, wdt_ref[...], preferred_element_type=F32) + dtb_ref[...]
            dt_ref[rows, :] = jnp.maximum(raw, 0.0) + jnp.log1p(jnp.exp(-jnp.abs(raw)))

    proj_ref[...] = jnp.dot(h_ref[...], w_ref[...], preferred_element_type=F32).astype(proj_ref.dtype)


def _in_proj(x, norm_w, w_main, w_dt, dt_bias, *, layer, tm, tn):
    m, d = x.shape
    n = w_main.shape[2]
    grid = (m // tm, n // tn)
    return pl.pallas_call(
        functools.partial(_inproj_kernel, row_chunk=min(256, tm)),
        grid=grid,
        in_specs=[
            pl.BlockSpec((tm, d), lambda i, j: (i, 0)),
            pl.BlockSpec((1, d), lambda i, j: (0, 0)),
            pl.BlockSpec((None, d, tn), lambda i, j: (layer, 0, j)),
            pl.BlockSpec((None, d, LANES), lambda i, j: (layer, 0, 0)),
            pl.BlockSpec((1, LANES), lambda i, j: (0, 0)),
        ],
        out_specs=[
            pl.BlockSpec((tm, tn), lambda i, j: (i, j)),
            pl.BlockSpec((tm, LANES), lambda i, j: (i, 0)),
        ],
        out_shape=[
            jax.ShapeDtypeStruct((m, n), BF16),
            jax.ShapeDtypeStruct((m, LANES), F32),
        ],
        scratch_shapes=[pltpu.VMEM((tm, d), BF16)],
        compiler_params=_compiler_params(("parallel", "arbitrary")),
        name="in_proj",
    )(x, norm_w, w_main, w_dt, dt_bias)


def _half_rms(x, w):
    lane = lax.broadcasted_iota(jnp.int32, x.shape, 1)
    left = lane < ATTN_HEAD_DIM
    x2 = x * x
    s_left = jnp.sum(jnp.where(left, x2, 0.0), axis=-1, keepdims=True)
    s_right = jnp.sum(jnp.where(left, 0.0, x2), axis=-1, keepdims=True)
    ms = jnp.where(left, s_left, s_right) * (1.0 / ATTN_HEAD_DIM)
    return x * lax.rsqrt(ms + EPS) * w


def _full_tile_schedule(n_tiles):
    return [(qi, kj) for qi in range(n_tiles) for kj in range(qi)]


def _attn_kernel(sched_ref, lam_ref, li_ref, q_ref, k_ref, v_ref, z_ref, qw_ref, kw_ref, sw_ref, o_ref,
                 kn_ref, vt_ref, qm_ref, s_ref, mt_ref, m_ref, acc_ref, *, tile, q_scale):
    n_tiles = q_ref.shape[0] // tile
    full_sched = _full_tile_schedule(n_tiles)
    n_full = len(full_sched)
    rows = lambda i: slice(i * tile, (i + 1) * tile)

    def prep(j):
        kn_ref[j] = _half_rms(k_ref[rows(j), :].astype(F32), kw_ref[...]).astype(BF16)
        vt = v_ref[rows(j), :].astype(F32).T
        ones_row = (lax.broadcasted_iota(jnp.int32, (PV_EXTRA_ROWS, tile), 0) == 0).astype(F32)
        vt_ref[j] = jnp.concatenate([vt, ones_row], axis=0).astype(BF16)
        q = (_half_rms(q_ref[rows(j), :].astype(F32), qw_ref[...]) * q_scale).astype(BF16)
        lane = lax.broadcasted_iota(jnp.int32, q.shape, 1)
        zero = jnp.zeros_like(q)
        qm_ref[j, 0] = jnp.where(lane < ATTN_HEAD_DIM, q, zero)
        qm_ref[j, 1] = jnp.where(lane < ATTN_HEAD_DIM, zero, q)
        m_ref[j] = jnp.full(m_ref.shape[1:], NEG_BIG, F32)
        acc_ref[j] = jnp.zeros(acc_ref.shape[1:], F32)

    def scores(qi, kj, slot, diag):
        k = kn_ref[kj]
        for mp in range(2):
            s = lax.dot_general(k, qm_ref[qi, mp], NT_DIMS, preferred_element_type=F32)
            if diag:
                key = lax.broadcasted_iota(jnp.int32, s.shape, 0)
                qry = lax.broadcasted_iota(jnp.int32, s.shape, 1)
                s = jnp.where(key <= qry, s, NEG_BIG)
            s_ref[slot, mp] = s.astype(BF16)
            mt_ref[slot, mp] = jnp.max(s, axis=0, keepdims=True)

    def softmax_pv(qi, kj, slot):
        vt = vt_ref[kj]
        for mp in range(2):
            m_old = m_ref[qi, mp]
            mt = mt_ref[slot, mp]
            mt_up = (mt + jnp.abs(mt) * BF16_ULP_UP).astype(BF16).astype(F32)
            m_new = jnp.maximum(m_old, mt_up)
            p = jnp.exp2(s_ref[slot, mp] - m_new.astype(BF16))
            alpha = jnp.exp2(m_old - m_new)
            acc_ref[qi, mp] = alpha * acc_ref[qi, mp] + jnp.dot(vt, p, preferred_element_type=F32)
            m_ref[qi, mp] = m_new

    for j in range(n_tiles + 1):
        if j < n_tiles:
            prep(j)
            scores(j, j, j % 2, True)
        if j >= 1:
            softmax_pv(j - 1, j - 1, (j - 1) % 2)

    if n_full:
        unroll = PHASE_B_UNROLL
        n_loop = (n_full - 1) // unroll
        scores(full_sched[0][0], full_sched[0][1], 0, False)

        def body(it, carry):
            base = it * unroll
            steps = [(sched_ref[0, base + u], sched_ref[1, base + u]) for u in range(unroll + 1)]
            for u in range(unroll):
                scores(*steps[u + 1], (u + 1) % 2, False)
                softmax_pv(*steps[u], u % 2)
            return carry

        lax.fori_loop(0, n_loop, body, 0)
        for t in range(n_loop * unroll, n_full):
            if t + 1 < n_full:
                scores(*full_sched[t + 1], (t + 1) % 2, False)
            softmax_pv(*full_sched[t], t % 2)

    lf = lam_ref[...]
    li = li_ref[:, 0:1]
    s01 = jnp.sum(lf[0:1] * lf[1:2], axis=-1, keepdims=True)
    s23 = jnp.sum(lf[2:3] * lf[3:4], axis=-1, keepdims=True)
    lam = jnp.exp(s01) - jnp.exp(s23) + li
    for j in range(n_tiles):
        a0, a1 = acc_ref[j, 0], acc_ref[j, 1]
        hd = ATTN_V_DIM
        o = a0[:hd] / a0[hd:hd + 1] - lam * (a1[:hd] / a1[hd:hd + 1])
        ms = jnp.mean(o * o, axis=0, keepdims=True)
        on = (o * lax.rsqrt(ms + EPS)).T
        z = z_ref[rows(j), :].astype(F32)
        y = on * sw_ref[...] * (1.0 - li) * _silu(z)
        o_ref[rows(j), :] = y.astype(o_ref.dtype)


def _diff_attn(diff_lambda, lam_init, proj, qw, kw, subln_w, *, batch, seq, heads, tile, q_col0, k_col0, v_col0,
               z_col0):
    m = proj.shape[0]
    n_tiles = seq // tile
    q_scale = ATTN_HEAD_DIM ** -0.5 * LOG2E
    sched = jnp.asarray(_full_tile_schedule(n_tiles) or [(0, 0)], jnp.int32).T
    const = lambda b, h, s: (0, 0)
    col = lambda c0: (lambda b, h, s: (b, c0 + h))
    blk = (seq, ATTN_V_DIM)
    grid_spec = pltpu.PrefetchScalarGridSpec(
        num_scalar_prefetch=1,
        grid=(batch, heads),
        in_specs=[
            pl.BlockSpec(diff_lambda.shape, const),
            pl.BlockSpec(lam_init.shape, const),
            pl.BlockSpec(blk, col(q_col0)),
            pl.BlockSpec(blk, col(k_col0)),
            pl.BlockSpec(blk, col(v_col0)),
            pl.BlockSpec(blk, col(z_col0)),
            pl.BlockSpec(qw.shape, const),
            pl.BlockSpec(kw.shape, const),
            pl.BlockSpec(subln_w.shape, const),
        ],
        out_specs=pl.BlockSpec(blk, lambda b, h, s: (b, h)),
        scratch_shapes=[
            pltpu.VMEM((n_tiles, tile, ATTN_V_DIM), BF16),
            pltpu.VMEM((n_tiles, ATTN_V_DIM + PV_EXTRA_ROWS, tile), BF16),
            pltpu.VMEM((n_tiles, 2, tile, ATTN_V_DIM), BF16),
            pltpu.VMEM((2, 2, tile, tile), BF16),
            pltpu.VMEM((2, 2, 1, tile), F32),
            pltpu.VMEM((n_tiles, 2, 1, tile), F32),
            pltpu.VMEM((n_tiles, 2, ATTN_V_DIM + PV_EXTRA_ROWS, tile), F32),
        ],
    )
    return pl.pallas_call(
        functools.partial(_attn_kernel, tile=tile, q_scale=q_scale),
        grid_spec=grid_spec,
        out_shape=jax.ShapeDtypeStruct((m, heads * ATTN_V_DIM), BF16),
        compiler_params=_compiler_params(("parallel", "parallel")),
        name="diff_attn",
    )(sched, diff_lambda, lam_init, proj, proj, proj, proj, qw, kw, subln_w)


def _causal_conv_silu(cur, prev, shift, w_ref, b_ref):
    rows = cur.shape[0]
    ext = jnp.concatenate([prev, cur], axis=0)
    shifted = jnp.dot(shift, ext, preferred_element_type=F32)
    acc = b_ref[...] + w_ref[0:1, :] * shifted[0:rows, :]
    for k in range(1, SSD_CONV):
        acc = acc + w_ref[k:k + 1, :] * shifted[k * rows:(k + 1) * rows, :]
    return _silu(acc)


def _split3(x):
    hi = x.astype(BF16)
    r1 = x - hi.astype(F32)
    mid = r1.astype(BF16)
    lo = (r1 - mid.astype(F32)).astype(BF16)
    return hi, mid, lo


def _ssd_chunk(xbc, zs, dt, a_neg, dskip_ref, nw_ref, state_ref, y_ref, out_rows):
    L = xbc.shape[0]
    N = SSD_STATE
    width = y_ref.shape[1]
    gn = SSD_GROUPS * N
    pair_w = 2 * SSD_HEAD_DIM
    pairs_per_group = width // pair_w // SSD_GROUPS
    xs = xbc[:, :width]
    bm = xbc[:, width:width + gn]
    cm = xbc[:, width + gn:]

    dta = dt * a_neg
    row = lax.broadcasted_iota(jnp.int32, (L, L), 0)
    col = lax.broadcasted_iota(jnp.int32, (L, L), 1)
    causal = row >= col
    tri = jnp.where(causal, 1.0, 0.0).astype(BF16)
    a_cum = sum(jnp.dot(tri, part, preferred_element_type=F32) for part in _split3(dta))
    a_cum_t = a_cum.T
    dt_t = dt.T
    a_src_t = a_cum_t - jnp.log2(dt_t)

    xs_bf = xs.astype(BF16)
    lane = lax.broadcasted_iota(jnp.int32, (L, pair_w), 1)
    left = lane < SSD_HEAD_DIM

    y_slabs = []
    for g in range(SSD_GROUPS):
        bg = bm[:, g * N:(g + 1) * N]
        cg = cm[:, g * N:(g + 1) * N]
        cb = lax.dot_general(cg.astype(BF16), bg.astype(BF16), NT_DIMS, preferred_element_type=F32)
        cb = jnp.where(causal, cb, 0.0)
        bg_t = bg.T
        for jp in range(pairs_per_group):
            pr = g * pairs_per_group + jp
            lanes = slice(pr * pair_w, (pr + 1) * pair_w)
            x_pair = xs_bf[:, lanes]
            s_pair = state_ref[:, lanes]
            lhs_y, lhs_s, chunk_decay = [], [], []
            for e in range(2):
                h = 2 * pr + e
                colb = jnp.broadcast_to(a_cum[:, h:h + 1], (L, L))
                rowb = a_cum_t[h:h + 1, :]
                m_h = cb * jnp.exp2(jnp.minimum(colb - a_src_t[h:h + 1, :], DECAY_EXP_CLAMP))
                cw_h = cg * jnp.exp2(colb)
                lhs_y.append(jnp.concatenate([m_h, cw_h], axis=1).astype(BF16))
                a_last = colb[L - 1:L, :]
                wrow = dt_t[h:h + 1, :] * jnp.exp2(a_last - rowb)
                lhs_s.append((bg_t * wrow).astype(BF16))
                chunk_decay.append(jnp.exp2(a_last))
            rhs_y = jnp.concatenate([x_pair, s_pair.astype(BF16)], axis=0)
            y2 = jnp.dot(jnp.concatenate(lhs_y, axis=0), rhs_y, preferred_element_type=F32)
            y_slabs.append(jnp.where(left, y2[:L], y2[L:]))
            s2 = jnp.dot(jnp.concatenate(lhs_s, axis=0), x_pair, preferred_element_type=F32)
            state_ref[:, lanes] = (jnp.where(left, s2[:N], s2[N:])
                                   + s_pair * jnp.where(left, chunk_decay[0], chunk_decay[1]))

    y = jnp.concatenate(y_slabs, axis=1) + xs * dskip_ref[...]
    gated = y * _silu(zs.astype(F32))
    gw = width // SSD_GROUPS
    for g in range(SSD_GROUPS):
        blk = gated[:, g * gw:(g + 1) * gw]
        ms = jnp.mean(blk * blk, axis=-1, keepdims=True)
        y_ref[out_rows, g * gw:(g + 1) * gw] = (blk * lax.rsqrt(ms + EPS)
                                                * nw_ref[:, g * gw:(g + 1) * gw]).astype(y_ref.dtype)


def _ssd_kernel(xbc_ref, zs_ref, dt_ref, cw_ref, cb_ref, alog_ref, dskip_ref, nw_ref, shift_ref,
                y_ref,
                prev_ref, state_ref):
    c = pl.program_id(1)
    L = SSD_CHUNK

    @pl.when(c == 0)
    def _():
        prev_ref[...] = jnp.zeros(prev_ref.shape, prev_ref.dtype)
        state_ref[...] = jnp.zeros(state_ref.shape, F32)

    a_neg = -jnp.exp(alog_ref[...]) * LOG2E
    shift = shift_ref[...]
    prev = prev_ref[...]
    for sub in range(xbc_ref.shape[0] // L):
        rows = slice(sub * L, (sub + 1) * L)
        cur = xbc_ref[rows, :]
        xbc = _causal_conv_silu(cur, prev, shift, cw_ref, cb_ref)
        _ssd_chunk(xbc, zs_ref[rows, :], dt_ref[rows, :], a_neg, dskip_ref, nw_ref, state_ref, y_ref, rows)
        prev = cur
    prev_ref[...] = prev


def _ssd(proj, dt, conv_w, conv_b, a_log, d_skip, norm_w, *, batch, seq, rows_per_step, xbc_col, zs_col):
    assert SSD_CHUNK == SSD_STATE == LANES, "the chunk kernel reuses (L, L) tiles as (L, N) and head rows as lanes"
    m = proj.shape[0]
    L = SSD_CHUNK
    R = rows_per_step
    assert R % L == 0 and seq % R == 0
    nc = seq // R
    conv_dim = conv_w.shape[1]
    width = norm_w.shape[1]
    row = lambda b, c: b * nc + c
    const = lambda b, c: (0, 0)
    tap_row = jnp.arange(SSD_CONV * L)[:, None]
    shift = (jnp.arange(2 * L)[None, :] == tap_row % L + L - (SSD_CONV - 1) + tap_row // L).astype(BF16)
    return pl.pallas_call(
        _ssd_kernel,
        grid=(batch, nc),
        in_specs=[
            pl.BlockSpec((R, conv_dim), lambda b, c: (row(b, c), xbc_col)),
            pl.BlockSpec((R, width), lambda b, c: (row(b, c), zs_col)),
            pl.BlockSpec((R, LANES), lambda b, c: (row(b, c), 0)),
            pl.BlockSpec(conv_w.shape, const), pl.BlockSpec(conv_b.shape, const),
            pl.BlockSpec(a_log.shape, const), pl.BlockSpec(d_skip.shape, const),
            pl.BlockSpec(norm_w.shape, const),
            pl.BlockSpec(shift.shape, const),
        ],
        out_specs=pl.BlockSpec((R, width), lambda b, c: (row(b, c), 0)),
        out_shape=jax.ShapeDtypeStruct((m, width), BF16),
        scratch_shapes=[
            pltpu.VMEM((L, conv_dim), BF16),
            pltpu.VMEM((SSD_STATE, width), F32),
        ],
        compiler_params=_compiler_params(("parallel", "arbitrary")),
        name="ssd",
    )(proj, proj, dt, conv_w, conv_b, a_log, d_skip, norm_w, shift)


def _outproj_kernel(ya_ref, ys_ref, ga_ref, gs_ref, x_ref, wpa_ref, wps_ref, wo_ref, o_ref):
    pa = jnp.dot(ya_ref[...], wpa_ref[...], preferred_element_type=F32)
    ps = jnp.dot(ys_ref[...], wps_ref[...], preferred_element_type=F32)
    merged = (jax.nn.sigmoid(ga_ref[...].astype(F32)) * pa
              + jax.nn.sigmoid(gs_ref[...].astype(F32)) * ps)
    o_ref[...] = x_ref[...] + jnp.dot(merged.astype(BF16), wo_ref[...], preferred_element_type=F32)


def _out_proj(y_a, y_s, proj, x, w_pa, w_ps, w_o, *, layer, tm, ga_col, gs_col):
    m, d = x.shape
    whole = lambda w: pl.BlockSpec((None,) + w.shape[1:], lambda i: (layer, 0, 0))
    return pl.pallas_call(
        _outproj_kernel,
        grid=(m // tm,),
        in_specs=[
            pl.BlockSpec((tm, y_a.shape[1]), lambda i: (i, 0)),
            pl.BlockSpec((tm, y_s.shape[1]), lambda i: (i, 0)),
            pl.BlockSpec((tm, d), lambda i: (i, ga_col)),
            pl.BlockSpec((tm, d), lambda i: (i, gs_col)),
            pl.BlockSpec((tm, d), lambda i: (i, 0)),
            whole(w_pa),
            whole(w_ps),
            whole(w_o),
        ],
        out_specs=pl.BlockSpec((tm, d), lambda i: (i, 0)),
        out_shape=jax.ShapeDtypeStruct((m, d), F32),
        compiler_params=_compiler_params(("parallel",)),
        name="out_proj",
    )(y_a, y_s, proj, proj, x, w_pa, w_ps, w_o)


def _lambda_init(layer_idx):
    return 0.8 - 0.6 * math.exp(-0.3 * layer_idx)


def kernel(x, norm_w, w_in, q_norm_w, k_norm_w, diff_lambda, subln_w, conv_w, conv_b, dt_bias, a_log, d_skip,
           ssd_norm_w, w_proj_attn, w_proj_ssd, w_out):
    batch, seq, d_model = x.shape
    depth = w_in.shape[0]
    attn_w = w_proj_attn.shape[1]
    ssd_w = w_proj_ssd.shape[1]
    ssd_heads = dt_bias.shape[1]
    heads = attn_w // ATTN_V_DIM
    gn = SSD_GROUPS * SSD_STATE

    o_za = 3 * attn_w
    o_xs = 4 * attn_w
    o_b = o_xs + ssd_w
    o_c = o_b + gn
    o_zs = o_c + gn
    o_dt = o_zs + ssd_w
    o_gate = o_dt + ssd_heads
    w_bf = w_in.astype(BF16)
    w_main = jnp.concatenate([w_bf[:, :, :o_xs], w_bf[:, :, o_zs:o_dt], w_bf[:, :, o_xs:o_zs],
                              w_bf[:, :, o_gate:]], axis=-1)
    w_dt = jnp.pad(w_bf[:, :, o_dt:o_gate], ((0, 0), (0, 0), (0, LANES - ssd_heads)))
    pad_h = ((0, 0), (0, LANES - ssd_heads))
    dt_bias_p = jnp.pad(dt_bias, pad_h)[:, None, :]
    a_log_p = jnp.pad(a_log, pad_h)[:, None, :]
    d_skip_x = jnp.repeat(d_skip, SSD_HEAD_DIM, axis=-1)[:, None, :]
    qw = jnp.tile(q_norm_w, (1, 2))[:, None, :]
    kw = jnp.tile(k_norm_w, (1, 2))[:, None, :]
    w_pa = w_proj_attn.astype(BF16)
    w_ps = w_proj_ssd.astype(BF16)
    w_o = w_out.astype(BF16)

    conv_dim = ssd_w + 2 * gn
    c_zs = o_xs // ssd_w
    c_xbc = (o_xs + ssd_w) // conv_dim
    c_ga = (o_xs + ssd_w + conv_dim) // d_model
    c_gs = c_ga + 1
    c_za = o_za // ATTN_V_DIM
    assert (o_xs + ssd_w) % conv_dim == 0 and o_xs % ssd_w == 0 and (o_xs + ssd_w + conv_dim) % d_model == 0

    rows = batch * seq
    xf = x.reshape(rows, d_model)
    for l in range(depth):
        proj, dt = _in_proj(xf, norm_w[l][None, :], w_main, w_dt, dt_bias_p[l], layer=l,
                            tm=min(IN_PROJ_TM, rows), tn=IN_PROJ_TN)
        lam_init = jnp.full((1, LANES), _lambda_init(l), F32)
        y_a = _diff_attn(diff_lambda[l], lam_init, proj, qw[l], kw[l], subln_w[l][None, :], batch=batch, seq=seq,
                         heads=heads, tile=min(ATTN_TILE, seq), q_col0=0, k_col0=heads, v_col0=2 * heads,
                         z_col0=c_za)
        y_s = _ssd(proj, dt, conv_w[l], conv_b[l][None, :], a_log_p[l], d_skip_x[l], ssd_norm_w[l][None, :],
                   batch=batch, seq=seq, rows_per_step=min(SSD_ROWS_PER_STEP, seq), xbc_col=c_xbc, zs_col=c_zs)
        xf = _out_proj(y_a, y_s, proj, xf, w_pa, w_ps, w_o, layer=l, tm=min(OUT_PROJ_TM, rows),
                       ga_col=c_ga, gs_col=c_gs)
    return xf.reshape(batch, seq, d_model)
```
